```python
import math
import functools
import jax
import jax.numpy as jnp
from jax import lax
import numpy as np

D_MODEL = 2048
BATCH = 8
SEQ = 2048
DEPTH = 1
DEC_BATCH = 32
DEC_SEQ = 4
PAST_LEN = 16384
PAGE_SIZE = 128

SB_HEAD_DIM = 128
SB_HEADS = D_MODEL // (2 * SB_HEAD_DIM)
SB_WIDTH = SB_HEADS * SB_HEAD_DIM
SB_BLOCK = 128
SB_BIAS_INIT = -6.0
DN_HEAD_DIM = 128
DN_HEADS = D_MODEL // (2 * DN_HEAD_DIM)
DN_WIDTH = DN_HEADS * DN_HEAD_DIM
DN_CONV = 4
DN_CHUNK = 64
OFF_SB_Q = 0
OFF_SB_K = OFF_SB_Q + SB_WIDTH
OFF_SB_V = OFF_SB_K + SB_WIDTH
OFF_DN_QKV = OFF_SB_V + SB_WIDTH
OFF_DN_Z = OFF_DN_QKV + 3 * DN_WIDTH
OFF_DN_B = OFF_DN_Z + DN_WIDTH
OFF_DN_A = OFF_DN_B + DN_HEADS
OFF_GATE = OFF_DN_A + DN_HEADS
N_IN = OFF_GATE + 2 * D_MODEL
N_EXPERTS = 32
TOP_K = 4
D_EXPERT = D_MODEL
SWIGLU_LIMIT = 7.0
SWIGLU_ALPHA = 1.702
MOE_BLOCK = 128
LN_EPS = 1e-5
RMS_EPS = 1e-6
L2_EPS = 1e-6
DEEPNORM_ALPHA = (2.0 * DEPTH) ** 0.25
DEEPNORM_BETA = (8.0 * DEPTH) ** -0.25

kernel_name = 'hybrid_stickbreak_gdn_moe_decode_step'


def _layer_norm(x, g, b):
    xf = x.astype(jnp.float32)
    mu = jnp.mean(xf, axis=-1, keepdims=True)
    xc = xf - mu
    var = jnp.mean(xc * xc, axis=-1, keepdims=True)
    return (xc * lax.rsqrt(var + LN_EPS) * g.astype(jnp.float32) + b.astype(jnp.float32)).astype(x.dtype)


def _l2_normalize(x):
    xf = x.astype(jnp.float32)
    return xf * lax.rsqrt(jnp.sum(xf * xf, axis=-1, keepdims=True) + L2_EPS)


def _gated_rms_norm(o, z, w):
    of = o.astype(jnp.float32)
    of = of * lax.rsqrt(jnp.mean(of * of, axis=-1, keepdims=True) + RMS_EPS)
    return (of * w.astype(jnp.float32) * jax.nn.silu(z.astype(jnp.float32))).astype(z.dtype)


def _stick_weights(z, mask, log_after):
    log_keep = jnp.where(mask, jax.nn.log_sigmoid(-z), 0.0)
    later = lax.cumsum(log_keep, axis=z.ndim - 1, reverse=True) - log_keep + log_after[..., None]
    w = jnp.where(mask, jnp.exp(jax.nn.log_sigmoid(z) + later), 0.0)
    return w, jnp.sum(log_keep, axis=-1)


def _sb_attend_prompt(q, k, v, bias):
    b, s, h, dh = q.shape
    qf = q.astype(jnp.float32) * dh ** -0.5
    kf = k.astype(jnp.float32)
    vf = v.astype(jnp.float32)
    bf = bias.astype(jnp.float32)[None, :, None, None]
    zero = jnp.zeros((b, h, SB_BLOCK), jnp.float32)
    outs = []
    for i in range(s // SB_BLOCK):
        t0, t1 = i * SB_BLOCK, (i + 1) * SB_BLOCK
        z = jnp.einsum('bqhd,bkhd->bhqk', qf[:, t0:t1], kf[:, :t1]) + bf
        mask = jnp.arange(t1)[None, :] < jnp.arange(t0, t1)[:, None]
        w, _ = _stick_weights(z, mask, zero)
        outs.append(jnp.einsum('bhqk,bkhd->bqhd', w, vf[:, :t1]))
    return jnp.concatenate(outs, axis=1).astype(q.dtype)


def _sb_attend_sample(q, k, v, bias, cache_k, cache_v, page_table, layer):
    b, nq, h, dh = q.shape
    qf = q.astype(jnp.float32) * dh ** -0.5
    bf = bias.astype(jnp.float32)[None, :, None, None]
    z = jnp.einsum('bqhd,bkhd->bhqk', qf, k.astype(jnp.float32)) + bf
    mask = jnp.arange(nq)[None, :] < jnp.arange(nq)[:, None]
    w, log_after = _stick_weights(z, mask, jnp.zeros((b, h, nq), jnp.float32))
    o = jnp.einsum('bhqk,bkhd->bqhd', w, v.astype(jnp.float32))

    def page_step(carry, pages):
        o_acc, log_later = carry
        kp = cache_k[layer, pages].astype(jnp.float32)
        vp = cache_v[layer, pages].astype(jnp.float32)
        zp = jnp.einsum('bqhd,bkhd->bhqk', qf, kp) + bf
        wp, log_keep = _stick_weights(zp, True, log_later)
        return (o_acc + jnp.einsum('bhqk,bkhd->bqhd', wp, vp), log_later + log_keep), None

    (o, _), _ = lax.scan(page_step, (o, log_after), page_table.T[::-1])
    return o.astype(q.dtype)


def _causal_conv(x, buf, w):
    xp = jnp.concatenate([buf.astype(x.dtype), x], axis=1)
    y = lax.conv_general_dilated(xp, w[:, None, :].astype(x.dtype), window_strides=(1,), padding='VALID',
                                 dimension_numbers=('NWC', 'WIO', 'NWC'), feature_group_count=x.shape[-1])
    return y, xp[:, -(DN_CONV - 1):]


def _gated_delta_rule(q, k, v, log_decay, beta, s0):
    b, s, h, dk = q.shape
    dv = v.shape[-1]
    c = math.gcd(s, DN_CHUNK)
    n = s // c
    f32 = jnp.float32

    def to_chunks(t):
        return t.astype(f32).reshape(b, n, c, h, -1).transpose(0, 1, 3, 2, 4)

    qc = to_chunks(q) * dk ** -0.5
    kc = to_chunks(k)
    vc = to_chunks(v)
    gc = log_decay.astype(f32).reshape(b, n, c, h).transpose(0, 1, 3, 2)
    bc = beta.astype(f32).reshape(b, n, c, h).transpose(0, 1, 3, 2)
    gcum = lax.cumsum(gc, axis=3)
    tri = jnp.tril(jnp.ones((c, c), dtype=bool))
    strict = jnp.tril(jnp.ones((c, c), dtype=bool), -1)
    decay = jnp.exp(jnp.where(tri, gcum[..., :, None] - gcum[..., None, :], -jnp.inf))
    a_low = jnp.where(strict, jnp.einsum('bnhcd,bnhed->bnhce', kc, kc) * decay, 0.0) * bc[..., :, None]
    rhs = jnp.concatenate([vc * bc[..., None], kc * (bc * jnp.exp(gcum))[..., None]], axis=-1)
    sol = lax.linalg.triangular_solve(jnp.eye(c, dtype=f32) + a_low, rhs, left_side=True, lower=True)
    u_beta, w_k = sol[..., :dv], sol[..., dv:]
    qk = jnp.einsum('bnhcd,bnhed->bnhce', qc, kc) * decay
    q_dec = qc * jnp.exp(gcum)[..., None]
    k_end = kc * jnp.exp(gcum[..., -1:] - gcum)[..., None]
    g_end = jnp.exp(gcum[..., -1])

    def chunk_step(state, xs):
        ub, wk, qk_c, qd, ke, ge = xs
        u = ub - jnp.einsum('bhcd,bhde->bhce', wk, state)
        o = jnp.einsum('bhcd,bhde->bhce', qd, state) + jnp.einsum('bhce,bhed->bhcd', qk_c, u)
        state = state * ge[..., None, None] + jnp.einsum('bhcd,bhce->bhde', ke, u)
        return state, o

    xs = tuple(jnp.moveaxis(t, 1, 0) for t in (u_beta, w_k, qk, q_dec, k_end, g_end))
    s_fin, o = lax.scan(chunk_step, s0.astype(f32), xs)
    o = jnp.moveaxis(o, 0, 1).transpose(0, 1, 3, 2, 4).reshape(b, s, h, dv)
    return o, s_fin.astype(s0.dtype)


def _moe_block_rows(n_assign):
    per_expert = max(1, n_assign // N_EXPERTS)
    return max(8, min(MOE_BLOCK, 1 << (per_expert.bit_length() - 1)))


def _moe_ffn(h, l, w_router, b_router, w_gu, b_gu, w_dn, b_dn):
    n, d = h.shape
    n_assign = n * TOP_K
    logits = h.astype(jnp.float32) @ w_router[l].astype(jnp.float32) + b_router[l].astype(jnp.float32)
    top_logit, top_e = lax.top_k(logits, TOP_K)
    gates = jax.nn.softmax(top_logit, axis=-1)
    blk = _moe_block_rows(n_assign)
    n_blocks = -(-n_assign // blk) + N_EXPERTS
    flat_e = top_e.reshape(-1).astype(jnp.int32)
    order = jnp.argsort(flat_e)
    sorted_e = flat_e[order]
    sorted_tok = (order // TOP_K).astype(jnp.int32)
    counts = jnp.bincount(flat_e, length=N_EXPERTS)
    padded = (counts + blk - 1) // blk * blk
    start = jnp.cumsum(counts) - counts
    pad_end = jnp.cumsum(padded)
    pad_start = pad_end - padded
    dest = (pad_start[sorted_e] + jnp.arange(n_assign) - start[sorted_e]).astype(jnp.int32)
    slot_tok = jnp.full((n_blocks * blk,), n, jnp.int32).at[dest].set(sorted_tok)
    block_e = jnp.minimum(jnp.searchsorted(pad_end, jnp.arange(n_blocks) * blk, side='right'), N_EXPERTS - 1)
    h_pad = jnp.concatenate([h, jnp.zeros((1, d), h.dtype)], axis=0)
    xb = h_pad[slot_tok].reshape(n_blocks, blk, d)

    def expert_block(args):
        x_blk, e = args
        gu = x_blk @ w_gu[l, e] + b_gu[l, e]
        gate = jnp.minimum(gu[:, :D_EXPERT], SWIGLU_LIMIT)
        up = jnp.clip(gu[:, D_EXPERT:], -SWIGLU_LIMIT, SWIGLU_LIMIT)
        act = (up + 1.0) * gate * jax.nn.sigmoid(gate * SWIGLU_ALPHA)
        return act @ w_dn[l, e] + b_dn[l, e]

    yb = lax.map(expert_block, (xb, block_e))
    slot_of = jnp.zeros((n_assign,), jnp.int32).at[order].set(dest)
    y = yb.reshape(-1, d)[slot_of].reshape(n, TOP_K, d)
    return jnp.einsum('nk,nkd->nd', gates.astype(h.dtype), y)


def _decoder_layer(x, c, l, attend, dn_s0, conv_buf, w_ada, b_ada, w_in, sb_bias, conv_w, dn_a_log, dn_dt_bias,
                   dn_norm_w, w_branch_sb, w_branch_dn, w_out, ln1_g, ln1_b, w_router, b_router,
                   w_gu, b_gu, w_dn, b_dn, ln2_g, ln2_b):
    b, s, d = x.shape
    f32 = jnp.float32
    mod = jax.nn.silu(c) @ w_ada[l] + b_ada[l]
    sh1, sc1, ga1, sh2, sc2, ga2 = (m[:, None, :] for m in jnp.split(mod, 6, axis=-1))
    h = x * (1.0 + sc1) + sh1
    proj = h @ w_in[l]
    q_sb = proj[..., OFF_SB_Q:OFF_SB_K].reshape(b, s, SB_HEADS, SB_HEAD_DIM)
    k_sb = proj[..., OFF_SB_K:OFF_SB_V].reshape(b, s, SB_HEADS, SB_HEAD_DIM)
    v_sb = proj[..., OFF_SB_V:OFF_DN_QKV].reshape(b, s, SB_HEADS, SB_HEAD_DIM)
    o_sb = attend(q_sb, k_sb, v_sb, sb_bias[l])
    qkv, conv_new = _causal_conv(proj[..., OFF_DN_QKV:OFF_DN_Z], conv_buf, conv_w[l])
    qkv = jax.nn.silu(qkv)
    q_dn, k_dn, v_dn = (t.reshape(b, s, DN_HEADS, DN_HEAD_DIM) for t in jnp.split(qkv, 3, axis=-1))
    beta = jax.nn.sigmoid(proj[..., OFF_DN_B:OFF_DN_A].astype(f32))
    log_decay = -jnp.exp(dn_a_log[l].astype(f32)) * jax.nn.softplus(
        proj[..., OFF_DN_A:OFF_GATE].astype(f32) + dn_dt_bias[l].astype(f32))
    o_dn, s_new = _gated_delta_rule(_l2_normalize(q_dn), _l2_normalize(k_dn), v_dn, log_decay, beta, dn_s0)
    z = proj[..., OFF_DN_Z:OFF_DN_B].reshape(b, s, DN_HEADS, DN_HEAD_DIM)
    o_dn = _gated_rms_norm(o_dn, z, dn_norm_w[l])
    g_sb = jax.nn.sigmoid(proj[..., OFF_GATE:OFF_GATE + D_MODEL])
    g_dn = jax.nn.sigmoid(proj[..., OFF_GATE + D_MODEL:N_IN])
    merged = (g_sb * (o_sb.reshape(b, s, SB_WIDTH) @ w_branch_sb[l])
              + g_dn * (o_dn.reshape(b, s, DN_WIDTH) @ w_branch_dn[l]))
    x = _layer_norm(DEEPNORM_ALPHA * x + ga1 * (merged @ w_out[l]), ln1_g[l], ln1_b[l])
    h = x * (1.0 + sc2) + sh2
    ffn = _moe_ffn(h.reshape(b * s, d), l, w_router, b_router, w_gu, b_gu, w_dn, b_dn).reshape(b, s, d)
    x = _layer_norm(DEEPNORM_ALPHA * x + ga2 * ffn, ln2_g[l], ln2_b[l])
    return x, k_sb, v_sb, s_new, conv_new


def setup_inputs(seed: int = 0) -> dict:
    key = jax.random.key(seed)
    ks = iter(jax.random.split(key, 40))
    f32 = jnp.float32

    def nrm(shape, scale):
        return jax.random.normal(next(ks), shape, f32) * scale

    n_pages = PAST_LEN // PAGE_SIZE
    n_phys = (DEC_BATCH * n_pages * 5) // 4
    x_prompt = nrm((BATCH, SEQ, D_MODEL), 1.0)
    x_sample = nrm((DEC_BATCH, DEC_SEQ, D_MODEL), 1.0)
    cache_k = nrm((DEPTH, n_phys, PAGE_SIZE, SB_HEADS, SB_HEAD_DIM), 1.0)
    cache_v = nrm((DEPTH, n_phys, PAGE_SIZE, SB_HEADS, SB_HEAD_DIM), 1.0)
    state_dn = nrm((DEPTH, DEC_BATCH, DN_HEADS, DN_HEAD_DIM, DN_HEAD_DIM), DN_HEAD_DIM ** -0.5)
    state_conv = nrm((DEPTH, DEC_BATCH, DN_CONV - 1, 3 * DN_WIDTH), 1.0)
    page_table = jax.random.permutation(next(ks), n_phys)[:DEC_BATCH * n_pages].reshape(
        DEC_BATCH, n_pages).astype(jnp.int32)
    c_prompt = nrm((BATCH, D_MODEL), 1.0)
    c_sample = nrm((DEC_BATCH, D_MODEL), 1.0)
    ln_in_g = 1.0 + nrm((D_MODEL,), 0.02)
    ln_in_b = nrm((D_MODEL,), 0.02)
    w_ada = nrm((DEPTH, D_MODEL, 6 * D_MODEL), D_MODEL ** -0.5)
    b_ada = nrm((DEPTH, 6 * D_MODEL), 0.02)
    col_scale = (jnp.ones((N_IN,), f32)
                 .at[OFF_SB_V:OFF_DN_QKV].set(DEEPNORM_BETA)
                 .at[OFF_DN_QKV + 2 * DN_WIDTH:OFF_DN_Z].set(DEEPNORM_BETA))
    w_in = nrm((DEPTH, D_MODEL, N_IN), D_MODEL ** -0.5) * col_scale
    sb_bias = SB_BIAS_INIT + nrm((DEPTH, SB_HEADS), 0.5)
    conv_w = nrm((DEPTH, DN_CONV, 3 * DN_WIDTH), DN_CONV ** -0.5)
    dn_a_log = jnp.log(jax.random.uniform(next(ks), (DEPTH, DN_HEADS), f32, 1.0, 16.0))
    dn_dt_bias = nrm((DEPTH, DN_HEADS), 0.1)
    dn_norm_w = 1.0 + nrm((DEPTH, DN_HEAD_DIM), 0.02)
    w_branch_sb = nrm((DEPTH, SB_WIDTH, D_MODEL), SB_WIDTH ** -0.5 * DEEPNORM_BETA)
    w_branch_dn = nrm((DEPTH, DN_WIDTH, D_MODEL), DN_WIDTH ** -0.5 * DEEPNORM_BETA)
    w_out = nrm((DEPTH, D_MODEL, D_MODEL), D_MODEL ** -0.5 * DEEPNORM_BETA)
    ln1_g = 1.0 + nrm((DEPTH, D_MODEL), 0.02)
    ln1_b = nrm((DEPTH, D_MODEL), 0.02)
    w_router = nrm((DEPTH, D_MODEL, N_EXPERTS), D_MODEL ** -0.5)
    b_router = nrm((DEPTH, N_EXPERTS), 0.01)
    w_gu = nrm((DEPTH, N_EXPERTS, D_MODEL, 2 * D_EXPERT), D_MODEL ** -0.5 * DEEPNORM_BETA)
    b_gu = nrm((DEPTH, N_EXPERTS, 2 * D_EXPERT), 0.02)
    w_dn = nrm((DEPTH, N_EXPERTS, D_EXPERT, D_MODEL), D_EXPERT ** -0.5 * DEEPNORM_BETA)
    b_dn = nrm((DEPTH, N_EXPERTS, D_MODEL), 0.02)
    ln2_g = 1.0 + nrm((DEPTH, D_MODEL), 0.02)
    ln2_b = nrm((DEPTH, D_MODEL), 0.02)
    return {'x_prompt': x_prompt, 'x_sample': x_sample, 'cache_k': cache_k, 'cache_v': cache_v,
            'state_dn': state_dn, 'state_conv': state_conv, 'page_table': page_table,
            'c_prompt': c_prompt, 'c_sample': c_sample, 'ln_in_g': ln_in_g, 'ln_in_b': ln_in_b,
            'w_ada': w_ada, 'b_ada': b_ada, 'w_in': w_in, 'sb_bias': sb_bias, 'conv_w': conv_w,
            'dn_a_log': dn_a_log, 'dn_dt_bias': dn_dt_bias, 'dn_norm_w': dn_norm_w, 'w_branch_sb': w_branch_sb,
            'w_branch_dn': w_branch_dn, 'w_out': w_out, 'ln1_g': ln1_g, 'ln1_b': ln1_b,
            'w_router': w_router, 'b_router': b_router, 'w_gu': w_gu, 'b_gu': b_gu,
            'w_dn': w_dn, 'b_dn': b_dn, 'ln2_g': ln2_g, 'ln2_b': ln2_b}


def reference(x_prompt, x_sample, cache_k, cache_v, state_dn, state_conv, page_table, c_prompt, c_sample,
              ln_in_g, ln_in_b, w_ada, b_ada, w_in, sb_bias, conv_w, dn_a_log, dn_dt_bias, dn_norm_w,
              w_branch_sb, w_branch_dn, w_out, ln1_g, ln1_b, w_router, b_router, w_gu, b_gu,
              w_dn, b_dn, ln2_g, ln2_b):
    weights = (w_ada, b_ada, w_in, sb_bias, conv_w, dn_a_log, dn_dt_bias, dn_norm_w, w_branch_sb, w_branch_dn,
               w_out, ln1_g, ln1_b, w_router, b_router, w_gu, b_gu, w_dn, b_dn, ln2_g, ln2_b)
    hp = _layer_norm(x_prompt, ln_in_g, ln_in_b)
    hs = _layer_norm(x_sample, ln_in_g, ln_in_b)
    bp = x_prompt.shape[0]
    new_p = []
    new_s = []
    for l in range(DEPTH):
        s0_p = jnp.zeros((bp, DN_HEADS, DN_HEAD_DIM, DN_HEAD_DIM), hp.dtype)
        buf_p = jnp.zeros((bp, DN_CONV - 1, 3 * DN_WIDTH), hp.dtype)
        hp, kp, vp, sp, cp = _decoder_layer(hp, c_prompt, l, _sb_attend_prompt, s0_p, buf_p, *weights)
        attend_s = functools.partial(_sb_attend_sample, cache_k=cache_k, cache_v=cache_v,
                                     page_table=page_table, layer=l)
        hs, ks_, vs_, ss_, cs_ = _decoder_layer(hs, c_sample, l, attend_s, state_dn[l], state_conv[l], *weights)
        new_p.append((kp, vp, sp, cp))
        new_s.append((ks_, vs_, ss_, cs_))
    k_prompt = jnp.stack([t[0] for t in new_p])
    v_prompt = jnp.stack([t[1] for t in new_p])
    state_dn_prompt = jnp.stack([t[2] for t in new_p])
    state_conv_prompt = jnp.stack([t[3] for t in new_p])
    k_sample = jnp.stack([t[0] for t in new_s])
    v_sample = jnp.stack([t[1] for t in new_s])
    state_dn_sample = jnp.stack([t[2] for t in new_s])
    state_conv_sample = jnp.stack([t[3] for t in new_s])
    return (hp, hs, k_prompt, v_prompt, state_dn_prompt, state_conv_prompt,
            k_sample, v_sample, state_dn_sample, state_conv_sample)
```

```python
import functools
import math

import jax
import jax.numpy as jnp
from jax import lax
from jax.experimental import pallas as pl
from jax.experimental.pallas import tpu as pltpu

F32 = jnp.float32
BF16 = jnp.bfloat16

D_MODEL = 2048
HEAD_DIM = 128
N_HEADS = 8
BRANCH_WIDTH = N_HEADS * HEAD_DIM
PAGE_SIZE = 128
DN_CONV = 4
DN_CHUNK = 64
N_EXPERTS = 32
TOP_K = 4
D_EXPERT = D_MODEL
SWIGLU_LIMIT = 7.0
SWIGLU_ALPHA = 1.702
LN_EPS = 1e-5
RMS_EPS = 1e-6
L2_EPS = 1e-6
DEPTH = 1
DEEPNORM_ALPHA = (2.0 * DEPTH) ** 0.25
OFF_DN_QKV = 3 * BRANCH_WIDTH
OFF_DN_Z = OFF_DN_QKV + 3 * BRANCH_WIDTH
OFF_DN_B = OFF_DN_Z + BRANCH_WIDTH
OFF_GATE = OFF_DN_B + 2 * N_HEADS
N_IN = OFF_GATE + 2 * D_MODEL

LANES = 128
SUBLANES = 8
VMEM_LIMIT = 48 * 1024 * 1024
MOE_TM = 512
MOE_TH = 512
MOE_TN = 1024
SAMPLE_PAGES_PER_STEP = 4
SAMPLE_QROWS = 16


def _params(*sem):
    return pltpu.CompilerParams(dimension_semantics=sem, vmem_limit_bytes=VMEM_LIMIT)


def _dot(a, b):
    return jnp.dot(a, b, preferred_element_type=F32)


def _dot_nt(a, b):
    return lax.dot_general(a, b, (((1,), (1,)), ((), ())), preferred_element_type=F32)


def _dot_tn(a, b):
    return lax.dot_general(a, b, (((0,), (0,)), ((), ())), preferred_element_type=F32)


def _split(x):
    hi = x.astype(BF16)
    lo = (x - hi.astype(F32)).astype(BF16)
    return hi, lo


def _dot3(a, b):
    ah, al = _split(a)
    bh, bl = _split(b)
    return _dot(ah, bh) + (_dot(ah, bl) + _dot(al, bh))


def _sigmoid(x):
    return 1.0 / (1.0 + jnp.exp(-x))


def _softplus(x):
    return jnp.maximum(x, 0.0) + jnp.log1p(jnp.exp(-jnp.abs(x)))


def _layer_norm(x, g, b):
    mu = jnp.mean(x, axis=-1, keepdims=True)
    xc = x - mu
    var = jnp.mean(xc * xc, axis=-1, keepdims=True)
    return xc * lax.rsqrt(var + LN_EPS) * g + b


class _Mod:
    def __init__(self, mod, rows_per_seq, n_rows):
        n_seq = mod.shape[0]
        self.grouped = rows_per_seq % SUBLANES == 0 and rows_per_seq >= LANES
        self.rows_per_seq = rows_per_seq
        m6 = mod.reshape(n_seq, 6, D_MODEL).transpose(1, 0, 2)
        if self.grouped:
            self.arr = m6.reshape(6, n_seq, 1, D_MODEL)
        else:
            self.arr = jnp.repeat(m6, rows_per_seq, axis=1)
            assert self.arr.shape[1] == n_rows

    def spec(self, comp, tm, row_block_of):
        if self.grouped:
            bps = self.rows_per_seq // tm
            return pl.BlockSpec((None, None, 1, D_MODEL),
                                lambda *g: (comp, row_block_of(*g) // bps, 0, 0))
        return pl.BlockSpec((None, tm, D_MODEL), lambda *g: (comp, row_block_of(*g), 0))


def _mm_kernel(x_ref, w_ref, *rest, has_bias, silu_in):
    if has_bias:
        b_ref, o_ref = rest
    else:
        (o_ref,) = rest
    x = x_ref[...]
    if silu_in:
        x = x * _sigmoid(x)
    acc = _dot(x.astype(BF16), w_ref[...].astype(BF16))
    if has_bias:
        acc = acc + b_ref[...]
    o_ref[...] = acc.astype(o_ref.dtype)


def _matmul(x, w, *, tm, tn, out_dtype, n_out, col0=0, bias=None, silu_in=False, name):
    n, k = x.shape
    assert n % tm == 0 and n_out % tn == 0 and col0 % tn == 0
    cb = col0 // tn
    in_specs = [pl.BlockSpec((tm, k), lambda i, j: (i, 0)),
                pl.BlockSpec((k, tn), lambda i, j: (0, j + cb))]
    args = [x, w]
    if bias is not None:
        in_specs.append(pl.BlockSpec((1, tn), lambda i, j: (0, j + cb)))
        args.append(bias)
    return pl.pallas_call(
        functools.partial(_mm_kernel, has_bias=bias is not None, silu_in=silu_in),
        out_shape=jax.ShapeDtypeStruct((n, n_out), out_dtype),
        grid=(n // tm, n_out // tn),
        in_specs=in_specs,
        out_specs=pl.BlockSpec((tm, tn), lambda i, j: (i, j)),
        compiler_params=_params("parallel", "arbitrary"),
        name=name,
    )(*args)


def _ln_mod_kernel(x_ref, g_ref, b_ref, sc_ref, sh_ref, xn_ref, h_ref):
    xn = _layer_norm(x_ref[...], g_ref[...], b_ref[...])
    xn_ref[...] = xn
    h_ref[...] = (xn * (1.0 + sc_ref[...]) + sh_ref[...]).astype(BF16)


def _ln_mod(x, g, b, mod, *, tm, name):
    n = x.shape[0]
    row = lambda i: i
    vec = pl.BlockSpec((1, D_MODEL), lambda i: (0, 0))
    blk = pl.BlockSpec((tm, D_MODEL), lambda i: (i, 0))
    return pl.pallas_call(
        _ln_mod_kernel,
        out_shape=(jax.ShapeDtypeStruct((n, D_MODEL), F32), jax.ShapeDtypeStruct((n, D_MODEL), BF16)),
        grid=(n // tm,),
        in_specs=[blk, vec, vec, mod.spec(1, tm, row), mod.spec(0, tm, row)],
        out_specs=(blk, blk),
        compiler_params=_params("parallel"),
        name=name,
    )(x, g.reshape(1, D_MODEL), b.reshape(1, D_MODEL), mod.arr, mod.arr)


def _stick_block(q, kb, vb, bias, log_after, later_mat, mask):
    z = _dot_nt(q, kb) + bias
    t = jnp.log1p(jnp.exp(-jnp.abs(z)))
    log_keep = -(jnp.maximum(z, 0.0) + t)
    log_beta = jnp.minimum(z, 0.0) - t
    if mask is not None:
        log_keep = jnp.where(mask, log_keep, 0.0)
    hi, lo = _split(log_keep)
    later = _dot(hi, later_mat) + _dot(lo, later_mat) + log_after
    w = jnp.exp(log_beta + later)
    if mask is not None:
        w = jnp.where(mask, w, 0.0)
    o = _dot(w.astype(BF16), vb)
    return o, log_after + jnp.sum(log_keep, axis=-1, keepdims=True)


def _later_matrix(tk):
    r = lax.broadcasted_iota(jnp.int32, (tk, tk), 0)
    c = lax.broadcasted_iota(jnp.int32, (tk, tk), 1)
    return (r > c).astype(BF16)


def _sb_prompt_kernel(bias_ref, q_ref, k_ref, v_ref, o_ref, *, t, scale):
    h = pl.program_id(1)
    qi = pl.program_id(2)
    bias = bias_ref[h]
    q = (q_ref[...] * scale).astype(BF16)
    later_mat = _later_matrix(t)
    r = lax.broadcasted_iota(jnp.int32, (t, t), 0)
    c = lax.broadcasted_iota(jnp.int32, (t, t), 1)

    def kv(j):
        rows = pl.ds(pl.multiple_of(j * t, t), t)
        return k_ref[rows, :].astype(BF16), v_ref[rows, :].astype(BF16)

    kb, vb = kv(qi)
    acc, log_after = _stick_block(q, kb, vb, bias, jnp.zeros((t, 1), F32), later_mat, c < r)

    def body(step, carry):
        acc, log_after = carry
        kb, vb = kv(qi - 1 - step)
        o, log_after = _stick_block(q, kb, vb, bias, log_after, later_mat, None)
        return acc + o, log_after

    acc, _ = lax.fori_loop(0, qi, body, (acc, log_after))
    o_ref[...] = acc.astype(o_ref.dtype)


def _sb_prompt(q, k, v, bias, *, n_seq, seq, t=256):
    nq = seq // t
    return pl.pallas_call(
        functools.partial(_sb_prompt_kernel, t=t, scale=HEAD_DIM ** -0.5),
        out_shape=jax.ShapeDtypeStruct(q.shape, BF16),
        grid=(n_seq, N_HEADS, nq),
        in_specs=[pl.BlockSpec(memory_space=pltpu.SMEM),
                  pl.BlockSpec((t, HEAD_DIM), lambda b, h, i: (b * nq + i, h)),
                  pl.BlockSpec((seq, HEAD_DIM), lambda b, h, i: (b, h)),
                  pl.BlockSpec((seq, HEAD_DIM), lambda b, h, i: (b, h))],
        out_specs=pl.BlockSpec((t, HEAD_DIM), lambda b, h, i: (b * nq + i, h)),
        compiler_params=_params("parallel", "parallel", "arbitrary"),
        name="sb_prompt",
    )(bias, q, k, v)


def _sb_sample_kernel(pt_ref, bias_ref, q_ref, kn_ref, vn_ref, *rest, n_pg, scale):
    k_refs = rest[:n_pg]
    v_refs = rest[n_pg:2 * n_pg]
    o_ref = rest[2 * n_pg]
    acc_ref, la_ref = rest[2 * n_pg + 1:]
    step = pl.program_id(1)
    qr = SAMPLE_QROWS
    rows = N_HEADS * qr
    later_mat = _later_matrix(PAGE_SIZE)
    bias = jnp.concatenate(
        [jnp.full((qr, 1), bias_ref[h], F32) for h in range(N_HEADS)], axis=0)
    q = (q_ref[...] * scale).astype(BF16)

    def page(k_ref, v_ref, mask):
        zs = []
        for h in range(N_HEADS):
            cols = slice(h * HEAD_DIM, (h + 1) * HEAD_DIM)
            zs.append(_dot_nt(q[h * qr:(h + 1) * qr], k_ref[:, cols].astype(BF16)))
        z = jnp.concatenate(zs, axis=0) + bias
        t = jnp.log1p(jnp.exp(-jnp.abs(z)))
        log_keep = -(jnp.maximum(z, 0.0) + t)
        log_beta = jnp.minimum(z, 0.0) - t
        if mask is not None:
            log_keep = jnp.where(mask, log_keep, 0.0)
        hi, lo = _split(log_keep)
        later = _dot(hi, later_mat) + _dot(lo, later_mat) + la_ref[...]
        w = jnp.exp(log_beta + later)
        if mask is not None:
            w = jnp.where(mask, w, 0.0)
        w = w.astype(BF16)
        for h in range(N_HEADS):
            cols = slice(h * HEAD_DIM, (h + 1) * HEAD_DIM)
            sl = slice(h * qr, (h + 1) * qr)
            acc_ref[sl, :] += _dot(w[sl], v_ref[:, cols].astype(BF16))
        la_ref[...] += jnp.sum(log_keep, axis=-1, keepdims=True)

    @pl.when(step == 0)
    def _():
        acc_ref[...] = jnp.zeros_like(acc_ref)
        la_ref[...] = jnp.zeros_like(la_ref)
        r = lax.broadcasted_iota(jnp.int32, (rows, PAGE_SIZE), 0) % qr
        c = lax.broadcasted_iota(jnp.int32, (rows, PAGE_SIZE), 1)
        page(kn_ref, vn_ref, c < r)

    @pl.when(step > 0)
    def _():
        for i in range(n_pg):
            page(k_refs[i], v_refs[i], None)

    @pl.when(step == pl.num_programs(1) - 1)
    def _():
        o_ref[...] = acc_ref[...]


def _sb_sample(q, k_new, v_new, bias, cache_k, cache_v, page_table):
    bd, n_pages = page_table.shape
    n_pg = SAMPLE_PAGES_PER_STEP
    assert n_pages % n_pg == 0
    n_steps = n_pages // n_pg
    width = N_HEADS * HEAD_DIM
    rows = N_HEADS * SAMPLE_QROWS

    def page_spec(i):
        def index(b, s, pt):
            logical = n_pages - 1 - (jnp.maximum(s - 1, 0) * n_pg + i)
            return (pt[b, logical], 0, 0)
        return pl.BlockSpec((None, PAGE_SIZE, width), index)

    new_spec = pl.BlockSpec((None, PAGE_SIZE, width), lambda b, s, pt: (b, 0, 0))
    q_spec = pl.BlockSpec((None, rows, HEAD_DIM), lambda b, s, pt: (b, 0, 0))
    grid_spec = pltpu.PrefetchScalarGridSpec(
        num_scalar_prefetch=1,
        grid=(bd, n_steps + 1),
        in_specs=[pl.BlockSpec(memory_space=pltpu.SMEM), q_spec, new_spec, new_spec]
        + [page_spec(i) for i in range(n_pg)] * 2,
        out_specs=q_spec,
        scratch_shapes=[pltpu.VMEM((rows, HEAD_DIM), F32), pltpu.VMEM((rows, 1), F32)],
    )
    return pl.pallas_call(
        functools.partial(_sb_sample_kernel, n_pg=n_pg, scale=HEAD_DIM ** -0.5),
        out_shape=jax.ShapeDtypeStruct((bd, rows, HEAD_DIM), F32),
        grid_spec=grid_spec,
        compiler_params=_params("parallel", "arbitrary"),
        name="sb_sample",
    )(page_table, bias, q, k_new, v_new, *([cache_k] * n_pg), *([cache_v] * n_pg))


def _dn_prep_kernel(xq_ref, xk_ref, xv_ref, pq_ref, pk_ref, pv_ref, wq_ref, wk_ref, wv_ref,
                    oq_ref, ok_ref, ov_ref, *, seq, valid):
    rows8 = lax.broadcasted_iota(jnp.int32, (SUBLANES, HEAD_DIM), 0)

    def conv_silu(x_ref, p_ref, w_ref):
        w = w_ref[...]
        x8 = x_ref[0:SUBLANES, :]
        p8 = p_ref[...]
        head = x8 * w[DN_CONV - 1:DN_CONV]
        for k in range(1, DN_CONV):
            shifted = jnp.where(rows8 < k, pltpu.roll(p8, k, 0), pltpu.roll(x8, k, 0))
            head = head + shifted * w[DN_CONV - 1 - k:DN_CONV - k]
        if seq == SUBLANES:
            acc = head
        else:
            x = x_ref[...]
            acc = x * w[DN_CONV - 1:DN_CONV]
            for k in range(1, DN_CONV):
                acc = acc + pltpu.roll(x, k, 0) * w[DN_CONV - 1 - k:DN_CONV - k]
            acc = jnp.concatenate([head, acc[SUBLANES:]], axis=0)
        y = acc * _sigmoid(acc)
        if valid < seq:
            r = lax.broadcasted_iota(jnp.int32, y.shape, 0)
            y = jnp.where(r < valid, y, 0.0)
        return y

    def l2n(y):
        return y * lax.rsqrt(jnp.sum(y * y, axis=-1, keepdims=True) + L2_EPS)

    oq_ref[...] = l2n(conv_silu(xq_ref, pq_ref, wq_ref)) * (HEAD_DIM ** -0.5)
    ok_ref[...] = l2n(conv_silu(xk_ref, pk_ref, wk_ref))
    ov_ref[...] = conv_silu(xv_ref, pv_ref, wv_ref)


def _dn_prep(x, prev, conv_w, *, n_seq, seq, valid, name):
    nh = N_HEADS
    xs = [pl.BlockSpec((seq, HEAD_DIM), lambda b, h, s=s: (b, s * nh + h)) for s in range(3)]
    ps = [pl.BlockSpec((None, SUBLANES, HEAD_DIM), lambda b, h, s=s: (b, 0, s * nh + h)) for s in range(3)]
    ws = [pl.BlockSpec((SUBLANES, HEAD_DIM), lambda b, h, s=s: (0, s * nh + h)) for s in range(3)]
    out = pl.BlockSpec((seq, HEAD_DIM), lambda b, h: (b, h))
    shape = jax.ShapeDtypeStruct((n_seq * seq, BRANCH_WIDTH), F32)
    return pl.pallas_call(
        functools.partial(_dn_prep_kernel, seq=seq, valid=valid),
        out_shape=(shape, shape, shape),
        grid=(n_seq, nh),
        in_specs=xs + ps + ws,
        out_specs=(out, out, out),
        compiler_params=_params("parallel", "parallel"),
        name=name,
    )(x, x, x, prev, prev, prev, conv_w, conv_w, conv_w)


def _unit_lower_inverse(a, c):
    r = lax.broadcasted_iota(jnp.int32, (c, c), 0)
    col = lax.broadcasted_iota(jnp.int32, (c, c), 1)
    p = jnp.where(r == col, 1.0, 0.0) - a
    power = a
    for _ in range(int(math.log2(c)) - 1):
        power = _dot3(power, power)
        p = p + _dot3(p, power)
    return p


def _dn_chunk_kernel(ascale_ref, dtb_ref, q_ref, k_ref, v_ref, z_ref, ba_ref, bat_ref, nw_ref, s0_ref,
                     o_ref, s_ref, *, chunk, n_chunks):
    c = chunk

    @pl.when(pl.program_id(1) == 0)
    def _():
        s_ref[...] = s0_ref[...]

    r = lax.broadcasted_iota(jnp.int32, (c, c), 0)
    col = lax.broadcasted_iota(jnp.int32, (c, c), 1)
    tri = r >= col
    strict = r > col
    nw = nw_ref[...]

    def one_chunk(ci, carry):
        rows = pl.ds(pl.multiple_of(ci * c, c), c)
        bat = bat_ref[ci]
        for h in range(N_HEADS):
            cols = slice(h * HEAD_DIM, (h + 1) * HEAD_DIM)
            q = q_ref[rows, cols]
            k = k_ref[rows, cols]
            v = v_ref[rows, cols]
            a_scale = ascale_ref[h]
            beta = _sigmoid(ba_ref[rows, h:h + 1])
            g_col = a_scale * _softplus(ba_ref[rows, N_HEADS + h:N_HEADS + h + 1] + dtb_ref[h])
            g_row = a_scale * _softplus(bat[N_HEADS + h:N_HEADS + h + 1, :] + dtb_ref[h])
            gc_col = jnp.sum(jnp.where(tri, jnp.broadcast_to(g_row, (c, c)), 0.0), axis=1, keepdims=True)
            gc_row = jnp.sum(jnp.where(r <= col, jnp.broadcast_to(g_col, (c, c)), 0.0), axis=0, keepdims=True)
            g_last = jnp.sum(g_row, axis=1, keepdims=True)
            decay = jnp.exp(jnp.where(tri, gc_col - gc_row, -1e30))
            kb = k.astype(BF16)
            a_low = jnp.where(strict, _dot_nt(kb, kb) * decay, 0.0) * beta
            t_inv = _unit_lower_inverse(a_low, c)
            e_col = jnp.exp(gc_col)
            rhs = jnp.concatenate([v * beta, k * (beta * e_col)], axis=1)
            sol = _dot3(t_inv, rhs)
            u_beta, w_k = sol[:, :HEAD_DIM], sol[:, HEAD_DIM:]
            qk = jnp.where(tri, _dot_nt(q.astype(BF16), kb) * decay, 0.0)
            q_dec = q * e_col
            k_end = k * jnp.exp(g_last - gc_col)
            state = s_ref[h]
            sb = state.astype(BF16)
            u = u_beta - _dot(w_k.astype(BF16), sb)
            o = _dot(q_dec.astype(BF16), sb) + _dot(qk.astype(BF16), u.astype(BF16))
            s_ref[h] = state * jnp.exp(g_last) + _dot_tn(k_end.astype(BF16), u.astype(BF16))
            o = o * lax.rsqrt(jnp.mean(o * o, axis=-1, keepdims=True) + RMS_EPS)
            zz = z_ref[rows, cols]
            o_ref[rows, cols] = (o * nw * (zz * _sigmoid(zz))).astype(o_ref.dtype)
        return carry

    lax.fori_loop(0, n_chunks, one_chunk, 0)


def _dn_chunks(q, k, v, z_arr, z_col0, ba, bat, a_scale, dt_bias, norm_w, s0, *, n_seq, seq, chunk, rows):
    nr = seq // rows
    ncb = rows // chunk
    w = BRANCH_WIDTH
    zb = z_col0 // w
    blk = pl.BlockSpec((rows, w), lambda b, r: (b * nr + r, 0))
    state = pl.BlockSpec((None, N_HEADS, HEAD_DIM, HEAD_DIM), lambda b, r: (b, 0, 0, 0))
    smem = pl.BlockSpec(memory_space=pltpu.SMEM)
    return pl.pallas_call(
        functools.partial(_dn_chunk_kernel, chunk=chunk, n_chunks=ncb),
        out_shape=(jax.ShapeDtypeStruct((n_seq * seq, w), BF16),
                   jax.ShapeDtypeStruct((n_seq, N_HEADS, HEAD_DIM, HEAD_DIM), F32)),
        grid=(n_seq, nr),
        in_specs=[smem, smem, blk, blk, blk,
                  pl.BlockSpec((rows, w), lambda b, r: (b * nr + r, zb)),
                  pl.BlockSpec((rows, LANES), lambda b, r: (b * nr + r, 0)),
                  pl.BlockSpec((ncb, 2 * N_HEADS, chunk), lambda b, r: (b * nr + r, 0, 0)),
                  pl.BlockSpec((1, HEAD_DIM), lambda b, r: (0, 0)),
                  state],
        out_specs=(blk, state),
        compiler_params=_params("parallel", "arbitrary"),
        name=f"dn_chunks_{chunk}",
    )(a_scale, dt_bias, q, k, v, z_arr, ba, bat, norm_w, s0)


def _merge_kernel(osb_ref, odn_ref, wsb_ref, wdn_ref, gsb_ref, gdn_ref, o_ref):
    a = _dot(osb_ref[...], wsb_ref[...])
    b = _dot(odn_ref[...], wdn_ref[...])
    o_ref[...] = (_sigmoid(gsb_ref[...]) * a + _sigmoid(gdn_ref[...]) * b).astype(o_ref.dtype)


def _merge(o_sb, o_dn, w_sb, w_dn, gates, *, tm, tn, name):
    n = o_sb.shape[0]
    w = BRANCH_WIDTH
    nj = D_MODEL // tn
    return pl.pallas_call(
        _merge_kernel,
        out_shape=jax.ShapeDtypeStruct((n, D_MODEL), BF16),
        grid=(n // tm, nj),
        in_specs=[pl.BlockSpec((tm, w), lambda i, j: (i, 0)),
                  pl.BlockSpec((tm, w), lambda i, j: (i, 0)),
                  pl.BlockSpec((w, tn), lambda i, j: (0, j)),
                  pl.BlockSpec((w, tn), lambda i, j: (0, j)),
                  pl.BlockSpec((tm, tn), lambda i, j: (i, j)),
                  pl.BlockSpec((tm, tn), lambda i, j: (i, j + nj))],
        out_specs=pl.BlockSpec((tm, tn), lambda i, j: (i, j)),
        compiler_params=_params("parallel", "arbitrary"),
        name=name,
    )(o_sb, o_dn, w_sb, w_dn, gates, gates)


def _out_ln_kernel(m_ref, w_ref, xn_ref, ga_ref, sc_ref, sh_ref, g_ref, b_ref, wr_ref, br_ref,
                   x1_ref, h_ref, lg_ref):
    y = _dot(m_ref[...], w_ref[...])
    x1 = _layer_norm(DEEPNORM_ALPHA * xn_ref[...] + ga_ref[...] * y, g_ref[...], b_ref[...])
    x1_ref[...] = x1
    h = x1 * (1.0 + sc_ref[...]) + sh_ref[...]
    h_ref[...] = h.astype(BF16)
    lg_ref[...] = _dot3(h, wr_ref[...]) + br_ref[...]


def _out_ln(merged, w_out, xn, mod, ln_g, ln_b, w_router, b_router, *, tm, name):
    n = merged.shape[0]
    row = lambda i: i
    blk = pl.BlockSpec((tm, D_MODEL), lambda i: (i, 0))
    vec = pl.BlockSpec((1, D_MODEL), lambda i: (0, 0))
    return pl.pallas_call(
        _out_ln_kernel,
        out_shape=(jax.ShapeDtypeStruct((n, D_MODEL), F32), jax.ShapeDtypeStruct((n, D_MODEL), BF16),
                   jax.ShapeDtypeStruct((n, LANES), F32)),
        grid=(n // tm,),
        in_specs=[blk, pl.BlockSpec((D_MODEL, D_MODEL), lambda i: (0, 0)), blk,
                  mod.spec(2, tm, row), mod.spec(4, tm, row), mod.spec(3, tm, row), vec, vec,
                  pl.BlockSpec((D_MODEL, LANES), lambda i: (0, 0)),
                  pl.BlockSpec((1, LANES), lambda i: (0, 0))],
        out_specs=(blk, blk, pl.BlockSpec((tm, LANES), lambda i: (i, 0))),
        compiler_params=_params("parallel"),
        name=name,
    )(merged, w_out, xn, mod.arr, mod.arr, mod.arr, ln_g, ln_b, w_router, b_router)


def _moe_up_kernel(be_ref, na_ref, x_ref, wg_ref, wu_ref, bg_ref, bu_ref, o_ref, wg_bf, wu_bf):
    m = pl.program_id(1)
    prev = be_ref[jnp.maximum(m - 1, 0)]
    new_weights = jnp.logical_or(m == 0, be_ref[m] != prev)

    @pl.when(m < na_ref[0])
    def _():
        @pl.when(new_weights)
        def _():
            wg_bf[...] = wg_ref[...].astype(BF16)
            wu_bf[...] = wu_ref[...].astype(BF16)

        x = x_ref[...]
        gate = jnp.minimum(_dot(x, wg_bf[...]) + bg_ref[...], SWIGLU_LIMIT)
        up = jnp.clip(_dot(x, wu_bf[...]) + bu_ref[...], -SWIGLU_LIMIT, SWIGLU_LIMIT)
        o_ref[...] = ((up + 1.0) * gate * _sigmoid(gate * SWIGLU_ALPHA)).astype(o_ref.dtype)

    @pl.when(m >= na_ref[0])
    def _():
        o_ref[...] = jnp.zeros_like(o_ref)


def _moe_down_kernel(be_ref, na_ref, a_ref, w_ref, b_ref, g_ref, o_ref, w_bf):
    m = pl.program_id(1)
    prev = be_ref[jnp.maximum(m - 1, 0)]
    new_weights = jnp.logical_or(m == 0, be_ref[m] != prev)

    @pl.when(m < na_ref[0])
    def _():
        @pl.when(new_weights)
        def _():
            w_bf[...] = w_ref[...].astype(BF16)

        y = _dot(a_ref[...], w_bf[...]) + b_ref[...]
        o_ref[...] = (y * g_ref[...]).astype(o_ref.dtype)

    @pl.when(m >= na_ref[0])
    def _():
        o_ref[...] = jnp.zeros_like(o_ref)


def _moe_experts(xs, block_e, n_active, row_gate, w_gu, b_gu, w_dn, b_dn):
    n_rows = xs.shape[0]
    nb = n_rows // MOE_TM
    nc = D_EXPERT // MOE_TH
    b_gu3 = b_gu.reshape(N_EXPERTS, 1, 2 * D_EXPERT)
    b_dn3 = b_dn.reshape(N_EXPERTS, 1, D_MODEL)
    act = pl.pallas_call(
        _moe_up_kernel,
        out_shape=jax.ShapeDtypeStruct((n_rows, D_EXPERT), BF16),
        grid_spec=pltpu.PrefetchScalarGridSpec(
            num_scalar_prefetch=2,
            grid=(nc, nb),
            in_specs=[pl.BlockSpec((MOE_TM, D_MODEL), lambda c, m, be, na: (m, 0)),
                      pl.BlockSpec((None, D_MODEL, MOE_TH), lambda c, m, be, na: (be[m], 0, c)),
                      pl.BlockSpec((None, D_MODEL, MOE_TH), lambda c, m, be, na: (be[m], 0, nc + c)),
                      pl.BlockSpec((None, 1, MOE_TH), lambda c, m, be, na: (be[m], 0, c)),
                      pl.BlockSpec((None, 1, MOE_TH), lambda c, m, be, na: (be[m], 0, nc + c))],
            out_specs=pl.BlockSpec((MOE_TM, MOE_TH), lambda c, m, be, na: (m, c)),
            scratch_shapes=[pltpu.VMEM((D_MODEL, MOE_TH), BF16), pltpu.VMEM((D_MODEL, MOE_TH), BF16)],
        ),
        compiler_params=_params("arbitrary", "arbitrary"),
        name="moe_up",
    )(block_e, n_active, xs, w_gu, w_gu, b_gu3, b_gu3)
    nj = D_MODEL // MOE_TN
    return pl.pallas_call(
        _moe_down_kernel,
        out_shape=jax.ShapeDtypeStruct((n_rows, D_MODEL), BF16),
        grid_spec=pltpu.PrefetchScalarGridSpec(
            num_scalar_prefetch=2,
            grid=(nj, nb),
            in_specs=[pl.BlockSpec((MOE_TM, D_EXPERT), lambda j, m, be, na: (m, 0)),
                      pl.BlockSpec((None, D_EXPERT, MOE_TN), lambda j, m, be, na: (be[m], 0, j)),
                      pl.BlockSpec((None, 1, MOE_TN), lambda j, m, be, na: (be[m], 0, j)),
                      pl.BlockSpec((MOE_TM, 1), lambda j, m, be, na: (m, 0))],
            out_specs=pl.BlockSpec((MOE_TM, MOE_TN), lambda j, m, be, na: (m, j)),
            scratch_shapes=[pltpu.VMEM((D_EXPERT, MOE_TN), BF16)],
        ),
        compiler_params=_params("arbitrary", "arbitrary"),
        name="moe_down",
    )(block_e, n_active, act, w_dn, b_dn3, row_gate)


def _route(logits, tm):
    n = logits.shape[0]
    n_assign = n * TOP_K
    n_blocks = -(-n_assign // tm) + N_EXPERTS
    top_logit, top_e = lax.top_k(logits, TOP_K)
    gates = jax.nn.softmax(top_logit, axis=-1).reshape(-1)
    flat_e = top_e.reshape(-1).astype(jnp.int32)
    onehot = (flat_e[:, None] == jnp.arange(N_EXPERTS, dtype=jnp.int32)[None, :]).astype(jnp.int32)
    running = jnp.cumsum(onehot, axis=0)
    counts = running[-1]
    rank = jnp.sum(running * onehot, axis=1) - 1
    padded = (counts + tm - 1) // tm * tm
    pad_end = jnp.cumsum(padded)
    pad_start = pad_end - padded
    start = jnp.cumsum(counts) - counts
    slot_of = pad_start[flat_e] + rank
    order = jnp.argsort(flat_e, stable=True).astype(jnp.int32)
    block_e = jnp.minimum(jnp.searchsorted(pad_end, jnp.arange(n_blocks, dtype=jnp.int32) * tm, side='right'),
                          N_EXPERTS - 1).astype(jnp.int32)
    slot = jnp.arange(n_blocks * tm, dtype=jnp.int32)
    slot_e = jnp.repeat(block_e, tm)
    slot_rank = slot - pad_start[slot_e]
    valid = (slot_rank >= 0) & (slot_rank < counts[slot_e])
    src = order[jnp.clip(start[slot_e] + slot_rank, 0, n_assign - 1)]
    slot_tok = jnp.where(valid, src // TOP_K, 0)
    row_gate = jnp.where(valid, gates[src], 0.0).astype(F32)
    n_active = (pad_end[-1] // tm).astype(jnp.int32).reshape(1)
    return slot_tok, row_gate.reshape(-1, 1), block_e, n_active, slot_of.reshape(n, TOP_K)


def _combine_ln_kernel(x1_ref, y_ref, ga_ref, g_ref, b_ref, o_ref):
    ffn = jnp.sum(y_ref[...].astype(F32), axis=0)
    o_ref[...] = _layer_norm(DEEPNORM_ALPHA * x1_ref[...] + ga_ref[...] * ffn, g_ref[...], b_ref[...])


def _combine_ln(x1, y4, row0, mod, ln_g, ln_b, *, tm, name):
    n = x1.shape[0]
    rb = row0 // tm
    row = lambda i: i
    blk = pl.BlockSpec((tm, D_MODEL), lambda i: (i, 0))
    vec = pl.BlockSpec((1, D_MODEL), lambda i: (0, 0))
    return pl.pallas_call(
        _combine_ln_kernel,
        out_shape=jax.ShapeDtypeStruct((n, D_MODEL), F32),
        grid=(n // tm,),
        in_specs=[blk, pl.BlockSpec((TOP_K, tm, D_MODEL), lambda i: (0, i + rb, 0)),
                  mod.spec(5, tm, row), vec, vec],
        out_specs=blk,
        compiler_params=_params("parallel"),
        name=name,
    )(x1, y4, mod.arr, ln_g, ln_b)


def _dense_front(x, mod, w, ln_in_g, ln_in_b, *, tm, tag):
    xn, h = _ln_mod(x, ln_in_g, ln_in_b, mod, tm=tm, name=f"ln_mod_{tag}")
    mm = functools.partial(_matmul, h, w["w_in"], tm=tm)
    q = mm(tn=1024, out_dtype=F32, n_out=BRANCH_WIDTH, col0=0, name=f"proj_q_{tag}")
    k = mm(tn=1024, out_dtype=F32, n_out=BRANCH_WIDTH, col0=BRANCH_WIDTH, name=f"proj_k_{tag}")
    v = mm(tn=1024, out_dtype=F32, n_out=BRANCH_WIDTH, col0=2 * BRANCH_WIDTH, name=f"proj_v_{tag}")
    dnz = mm(tn=1024, out_dtype=F32, n_out=4 * BRANCH_WIDTH, col0=3 * BRANCH_WIDTH, name=f"proj_dn_{tag}")
    gates = mm(tn=1024, out_dtype=F32, n_out=2 * D_MODEL, col0=7 * BRANCH_WIDTH, name=f"proj_gate_{tag}")
    ba = mm(tn=LANES, out_dtype=F32, n_out=LANES, col0=7 * BRANCH_WIDTH + 2 * D_MODEL, name=f"proj_ba_{tag}")
    return xn, q, k, v, dnz, gates, ba


def _dense_back(o_sb, o_dn, gates, xn, mod, w, *, tm, tm_out, tag):
    merged = _merge(o_sb, o_dn, w["w_branch_sb"], w["w_branch_dn"], gates, tm=tm, tn=1024, name=f"merge_{tag}")
    return _out_ln(merged, w["w_out"], xn, mod, w["ln1_g"], w["ln1_b"], w["w_router"], w["b_router"],
                   tm=tm_out, name=f"out_ln_{tag}")


def _logit_rows(ba, n_seq, seq, chunk):
    t = ba[:, :2 * N_HEADS].reshape(n_seq * seq // chunk, chunk, 2 * N_HEADS)
    return t.transpose(0, 2, 1)


def kernel(x_prompt, x_sample, cache_k, cache_v, state_dn, state_conv, page_table, c_prompt, c_sample,
           ln_in_g, ln_in_b, w_ada, b_ada, w_in, sb_bias, conv_w, dn_a_log, dn_dt_bias, dn_norm_w,
           w_branch_sb, w_branch_dn, w_out, ln1_g, ln1_b, w_router, b_router, w_gu, b_gu,
           w_dn, b_dn, ln2_g, ln2_b):
    assert w_ada.shape[0] == DEPTH == 1
    bp, sp, d = x_prompt.shape
    bs, ss, _ = x_sample.shape
    n_p, n_s = bp * sp, bs * ss
    l = 0

    w_in_l = w_in[l]
    w_in_r = jnp.concatenate(
        [w_in_l[:, :OFF_DN_B], w_in_l[:, OFF_GATE:], w_in_l[:, OFF_DN_B:OFF_GATE],
         jnp.zeros((d, LANES - 2 * N_HEADS), w_in.dtype)], axis=1).astype(BF16)
    w = {
        "w_in": w_in_r,
        "w_branch_sb": w_branch_sb[l].astype(BF16), "w_branch_dn": w_branch_dn[l].astype(BF16),
        "w_out": w_out[l].astype(BF16),
        "ln1_g": ln1_g[l].reshape(1, d), "ln1_b": ln1_b[l].reshape(1, d),
        "w_router": jnp.pad(w_router[l], ((0, 0), (0, LANES - N_EXPERTS))),
        "b_router": jnp.pad(b_router[l], (0, LANES - N_EXPERTS)).reshape(1, LANES),
    }
    ln2g, ln2b = ln2_g[l].reshape(1, d), ln2_b[l].reshape(1, d)
    conv_w8 = jnp.pad(conv_w[l], ((0, SUBLANES - DN_CONV), (0, 0)))
    norm_w = dn_norm_w[l].reshape(1, HEAD_DIM)
    a_scale = -jnp.exp(dn_a_log[l].astype(F32))

    c_all = jnp.concatenate([c_prompt, c_sample], axis=0)
    mod_all = _matmul(c_all, w_ada[l], tm=c_all.shape[0], tn=1024, out_dtype=F32, n_out=6 * d,
                      bias=b_ada[l].reshape(1, 6 * d), silu_in=True, name="ada")
    mod_p = _Mod(mod_all[:bp], sp, n_p)
    mod_s = _Mod(mod_all[bp:], ss, n_s)

    xn_p, q_p, k_p, v_p, dnz_p, gates_p, ba_p = _dense_front(
        x_prompt.reshape(n_p, d), mod_p, w, ln_in_g, ln_in_b, tm=min(1024, n_p), tag="p")
    o_sb_p = _sb_prompt(q_p, k_p, v_p, sb_bias[l], n_seq=bp, seq=sp)
    zero_prev = jnp.zeros((bp, SUBLANES, 3 * BRANCH_WIDTH), F32)
    dq, dk, dv = _dn_prep(dnz_p, zero_prev, conv_w8, n_seq=bp, seq=sp, valid=sp, name="dn_prep_p")
    o_dn_p, s_p = _dn_chunks(dq, dk, dv, dnz_p, 3 * BRANCH_WIDTH, ba_p, _logit_rows(ba_p, bp, sp, DN_CHUNK),
                             a_scale, dn_dt_bias[l], norm_w,
                             jnp.zeros((bp, N_HEADS, HEAD_DIM, HEAD_DIM), F32),
                             n_seq=bp, seq=sp, chunk=DN_CHUNK, rows=4 * DN_CHUNK)
    x1_p, h2_p, lg_p = _dense_back(o_sb_p, o_dn_p, gates_p, xn_p, mod_p, w, tm=min(1024, n_p), tm_out=256,
                                   tag="p")

    xn_s, q_s, k_s, v_s, dnz_s, gates_s, ba_s = _dense_front(
        x_sample.reshape(n_s, d), mod_s, w, ln_in_g, ln_in_b, tm=n_s, tag="s")
    pad_q = SUBLANES - ss

    def heads_rows(t):
        t = t.reshape(bs, ss, N_HEADS, HEAD_DIM).transpose(0, 2, 1, 3)
        return (jnp.pad(t, ((0, 0), (0, 0), (0, SAMPLE_QROWS - ss), (0, 0)))
                .reshape(bs, N_HEADS * SAMPLE_QROWS, HEAD_DIM))

    def new_page(t):
        return jnp.pad(t.reshape(bs, ss, BRANCH_WIDTH), ((0, 0), (0, PAGE_SIZE - ss), (0, 0)))

    n_phys = cache_k.shape[1]
    o_s = _sb_sample(heads_rows(q_s), new_page(k_s), new_page(v_s), sb_bias[l],
                     cache_k[l].reshape(n_phys, PAGE_SIZE, BRANCH_WIDTH),
                     cache_v[l].reshape(n_phys, PAGE_SIZE, BRANCH_WIDTH), page_table)
    o_sb_s = (o_s.reshape(bs, N_HEADS, SAMPLE_QROWS, HEAD_DIM)[:, :, :ss].transpose(0, 2, 1, 3)
              .reshape(n_s, BRANCH_WIDTH).astype(BF16))

    def pad_rows(t):
        return jnp.pad(t.reshape(bs, ss, -1), ((0, 0), (0, pad_q), (0, 0))).reshape(bs * SUBLANES, -1)

    dnz_s8 = pad_rows(dnz_s)
    ba_s8 = pad_rows(ba_s)
    prev_s = jnp.pad(state_conv[l], ((0, 0), (SUBLANES - (DN_CONV - 1), 0), (0, 0)))
    dq, dk, dv = _dn_prep(dnz_s8, prev_s, conv_w8, n_seq=bs, seq=SUBLANES, valid=ss, name="dn_prep_s")
    tok = jnp.arange(bs * SUBLANES) % SUBLANES
    neutral = jnp.concatenate([jnp.full((N_HEADS,), -1e30, F32), jnp.full((N_HEADS,), -1e30, F32),
                               jnp.zeros((LANES - 2 * N_HEADS,), F32)])
    ba_s8 = jnp.where((tok < ss)[:, None], ba_s8, neutral[None, :])
    o_dn_s8, s_s = _dn_chunks(dq, dk, dv, dnz_s8, 3 * BRANCH_WIDTH, ba_s8, _logit_rows(ba_s8, bs, SUBLANES, SUBLANES),
                              a_scale, dn_dt_bias[l], norm_w, state_dn[l],
                              n_seq=bs, seq=SUBLANES, chunk=SUBLANES, rows=SUBLANES)
    o_dn_s = o_dn_s8.reshape(bs, SUBLANES, BRANCH_WIDTH)[:, :ss].reshape(n_s, BRANCH_WIDTH)
    x1_s, h2_s, lg_s = _dense_back(o_sb_s, o_dn_s, gates_s, xn_s, mod_s, w, tm=n_s, tm_out=n_s, tag="s")

    logits = jnp.concatenate([lg_p[:, :N_EXPERTS], lg_s[:, :N_EXPERTS]], axis=0)
    h2 = jnp.concatenate([h2_p, h2_s], axis=0)
    slot_tok, row_gate, block_e, n_active, slot_of = _route(logits, MOE_TM)
    xs = jnp.take(h2, slot_tok, axis=0)
    yb = _moe_experts(xs, block_e, n_active, row_gate, w_gu[l], b_gu[l], w_dn[l], b_dn[l])
    y4 = jnp.take(yb, slot_of.T, axis=0)
    y_p = _combine_ln(x1_p, y4, 0, mod_p, ln2g, ln2b, tm=256, name="combine_ln_p")
    y_s = _combine_ln(x1_s, y4, n_p, mod_s, ln2g, ln2b, tm=n_s, name="combine_ln_s")

    heads = lambda t, b, s: t.reshape(1, b, s, N_HEADS, HEAD_DIM)
    conv_p = dnz_p.reshape(bp, sp, 4 * BRANCH_WIDTH)[:, sp - (DN_CONV - 1):, :3 * BRANCH_WIDTH]
    xp_s = jnp.concatenate([state_conv[l], dnz_s.reshape(bs, ss, 4 * BRANCH_WIDTH)[:, :, :3 * BRANCH_WIDTH]], axis=1)
    conv_s = xp_s[:, -(DN_CONV - 1):]
    return (y_p.reshape(bp, sp, d), y_s.reshape(bs, ss, d),
            heads(k_p, bp, sp), heads(v_p, bp, sp), s_p[None], conv_p[None],
            heads(k_s, bs, ss), heads(v_s, bs, ss), s_s[None], conv_s[None])
```

```python
import functools
import math

import jax
import jax.numpy as jnp
from jax import lax
from jax.experimental import pallas as pl
from jax.experimental.pallas import tpu as pltpu

F32 = jnp.float32
BF16 = jnp.bfloat16

D_MODEL = 2048
HEAD_DIM = 128
N_HEADS = 8
BRANCH_WIDTH = N_HEADS * HEAD_DIM
PAGE_SIZE = 128
DN_CONV = 4
DN_CHUNK = 64
N_EXPERTS = 32
TOP_K = 4
D_EXPERT = D_MODEL
SWIGLU_LIMIT = 7.0
SWIGLU_ALPHA = 1.702
LN_EPS = 1e-5
RMS_EPS = 1e-6
L2_EPS = 1e-6
DEPTH = 1
DEEPNORM_ALPHA = (2.0 * DEPTH) ** 0.25
OFF_DN_QKV = 3 * BRANCH_WIDTH
OFF_DN_Z = OFF_DN_QKV + 3 * BRANCH_WIDTH
OFF_DN_B = OFF_DN_Z + BRANCH_WIDTH
OFF_GATE = OFF_DN_B + 2 * N_HEADS
N_IN = OFF_GATE + 2 * D_MODEL

LANES = 128
SUBLANES = 8
VMEM_LIMIT = 48 * 1024 * 1024
MOE_TM = 512
MOE_TH = 512
MOE_TN = 1024
SB_PROMPT_TILE = 256
SB_PROMPT_HEADS = 2
SAMPLE_PAGES_PER_STEP = 4
SAMPLE_QROWS = 16


def _params(*sem):
    return pltpu.CompilerParams(dimension_semantics=sem, vmem_limit_bytes=VMEM_LIMIT)


def _dot(a, b):
    return jnp.dot(a, b, preferred_element_type=F32)


def _dot_nt(a, b):
    return lax.dot_general(a, b, (((1,), (1,)), ((), ())), preferred_element_type=F32)


def _dot_tn(a, b):
    return lax.dot_general(a, b, (((0,), (0,)), ((), ())), preferred_element_type=F32)


def _split(x):
    hi = x.astype(BF16)
    lo = (x - hi.astype(F32)).astype(BF16)
    return hi, lo


def _dot3(a, b):
    ah, al = _split(a)
    bh, bl = _split(b)
    return _dot(ah, bh) + (_dot(ah, bl) + _dot(al, bh))


def _sigmoid(x):
    return 1.0 / (1.0 + jnp.exp(-x))


def _softplus(x):
    return jnp.maximum(x, 0.0) + jnp.log1p(jnp.exp(-jnp.abs(x)))


def _layer_norm(x, g, b):
    mu = jnp.mean(x, axis=-1, keepdims=True)
    xc = x - mu
    var = jnp.mean(xc * xc, axis=-1, keepdims=True)
    return xc * lax.rsqrt(var + LN_EPS) * g + b


class _Mod:
    def __init__(self, mod, rows_per_seq, n_rows):
        n_seq = mod.shape[0]
        self.grouped = rows_per_seq % SUBLANES == 0 and rows_per_seq >= LANES
        self.rows_per_seq = rows_per_seq
        m6 = mod.reshape(n_seq, 6, D_MODEL).transpose(1, 0, 2)
        if self.grouped:
            self.arr = m6.reshape(6, n_seq, 1, D_MODEL)
        else:
            self.arr = jnp.repeat(m6, rows_per_seq, axis=1)
            assert self.arr.shape[1] == n_rows

    def spec(self, comp, tm, row_block_of):
        if self.grouped:
            bps = self.rows_per_seq // tm
            return pl.BlockSpec((None, None, 1, D_MODEL),
                                lambda *g: (comp, row_block_of(*g) // bps, 0, 0))
        return pl.BlockSpec((None, tm, D_MODEL), lambda *g: (comp, row_block_of(*g), 0))


def _mm_kernel(x_ref, w_ref, *rest, has_bias, silu_in):
    if has_bias:
        b_ref, o_ref = rest
    else:
        (o_ref,) = rest
    x = x_ref[...]
    if silu_in:
        x = x * _sigmoid(x)
    acc = _dot(x.astype(BF16), w_ref[...].astype(BF16))
    if has_bias:
        acc = acc + b_ref[...]
    o_ref[...] = acc.astype(o_ref.dtype)


def _matmul(x, w, *, tm, tn, out_dtype, n_out, col0=0, bias=None, silu_in=False, name):
    n, k = x.shape
    assert n % tm == 0 and n_out % tn == 0 and col0 % tn == 0
    cb = col0 // tn
    in_specs = [pl.BlockSpec((tm, k), lambda i, j: (i, 0)),
                pl.BlockSpec((k, tn), lambda i, j: (0, j + cb))]
    args = [x, w]
    if bias is not None:
        in_specs.append(pl.BlockSpec((1, tn), lambda i, j: (0, j + cb)))
        args.append(bias)
    return pl.pallas_call(
        functools.partial(_mm_kernel, has_bias=bias is not None, silu_in=silu_in),
        out_shape=jax.ShapeDtypeStruct((n, n_out), out_dtype),
        grid=(n // tm, n_out // tn),
        in_specs=in_specs,
        out_specs=pl.BlockSpec((tm, tn), lambda i, j: (i, j)),
        compiler_params=_params("parallel", "arbitrary"),
        name=name,
    )(*args)


def _ln_mod_kernel(x_ref, g_ref, b_ref, sc_ref, sh_ref, xn_ref, h_ref):
    xn = _layer_norm(x_ref[...], g_ref[...], b_ref[...])
    xn_ref[...] = xn
    h_ref[...] = (xn * (1.0 + sc_ref[...]) + sh_ref[...]).astype(BF16)


def _ln_mod(x, g, b, mod, *, tm, name):
    n = x.shape[0]
    row = lambda i: i
    vec = pl.BlockSpec((1, D_MODEL), lambda i: (0, 0))
    blk = pl.BlockSpec((tm, D_MODEL), lambda i: (i, 0))
    return pl.pallas_call(
        _ln_mod_kernel,
        out_shape=(jax.ShapeDtypeStruct((n, D_MODEL), F32), jax.ShapeDtypeStruct((n, D_MODEL), BF16)),
        grid=(n // tm,),
        in_specs=[blk, vec, vec, mod.spec(1, tm, row), mod.spec(0, tm, row)],
        out_specs=(blk, blk),
        compiler_params=_params("parallel"),
        name=name,
    )(x, g.reshape(1, D_MODEL), b.reshape(1, D_MODEL), mod.arr, mod.arr)


def _stick_blocks(qs, kbs, vbs, biases, log_afters, later_mat, mask):
    n = range(len(qs))
    zs = [_dot_nt(qs[i], kbs[i]) + biases[i] for i in n]
    log_keep, log_beta = [], []
    for z in zs:
        t = jnp.log1p(jnp.exp(-jnp.abs(z)))
        lk = -(jnp.maximum(z, 0.0) + t)
        log_keep.append(lk if mask is None else jnp.where(mask, lk, 0.0))
        log_beta.append(jnp.minimum(z, 0.0) - t)
    parts = [_split(lk) for lk in log_keep]
    within = [_dot(hi, later_mat) + _dot(lo, later_mat) for hi, lo in parts]
    ws = []
    for i in n:
        w = jnp.exp(log_beta[i] + (within[i] + log_afters[i]))
        ws.append((w if mask is None else jnp.where(mask, w, 0.0)).astype(BF16))
    outs = [_dot(ws[i], vbs[i]) for i in n]
    new_after = [log_afters[i] + jnp.sum(log_keep[i], axis=-1, keepdims=True) for i in n]
    return outs, new_after


def _later_matrix(tk):
    r = lax.broadcasted_iota(jnp.int32, (tk, tk), 0)
    c = lax.broadcasted_iota(jnp.int32, (tk, tk), 1)
    return (r > c).astype(BF16)


def _sb_prompt_kernel(bias_ref, q_ref, k_ref, v_ref, o_ref, *, t, hp, scale):
    hg = pl.program_id(1)
    qi = pl.program_id(2)
    heads = range(hp)
    hcols = [slice(i * HEAD_DIM, (i + 1) * HEAD_DIM) for i in heads]
    biases = [bias_ref[hg * hp + i] for i in heads]
    qs = [(q_ref[:, hc] * scale).astype(BF16) for hc in hcols]
    later_mat = _later_matrix(t)
    r = lax.broadcasted_iota(jnp.int32, (t, t), 0)
    c = lax.broadcasted_iota(jnp.int32, (t, t), 1)

    def kv(j):
        rows = pl.ds(pl.multiple_of(j * t, t), t)
        return ([k_ref[rows, hc].astype(BF16) for hc in hcols],
                [v_ref[rows, hc].astype(BF16) for hc in hcols])

    kbs, vbs = kv(qi)
    zero = [jnp.zeros((t, 1), F32) for _ in heads]
    accs, afters = _stick_blocks(qs, kbs, vbs, biases, zero, later_mat, c < r)

    def body(step, carry):
        accs, afters = carry
        kbs, vbs = kv(qi - 1 - step)
        outs, afters = _stick_blocks(qs, kbs, vbs, biases, list(afters), later_mat, None)
        return tuple(a + o for a, o in zip(accs, outs)), tuple(afters)

    accs, _ = lax.fori_loop(0, qi, body, (tuple(accs), tuple(afters)))
    for i in heads:
        o_ref[:, hcols[i]] = accs[i].astype(o_ref.dtype)


def _sb_prompt(q, k, v, bias, *, n_seq, seq, t=SB_PROMPT_TILE, hp=SB_PROMPT_HEADS):
    nq = seq // t
    w = hp * HEAD_DIM
    return pl.pallas_call(
        functools.partial(_sb_prompt_kernel, t=t, hp=hp, scale=HEAD_DIM ** -0.5),
        out_shape=jax.ShapeDtypeStruct(q.shape, BF16),
        grid=(n_seq, N_HEADS // hp, nq),
        in_specs=[pl.BlockSpec(memory_space=pltpu.SMEM),
                  pl.BlockSpec((t, w), lambda b, h, i: (b * nq + i, h)),
                  pl.BlockSpec((seq, w), lambda b, h, i: (b, h)),
                  pl.BlockSpec((seq, w), lambda b, h, i: (b, h))],
        out_specs=pl.BlockSpec((t, w), lambda b, h, i: (b * nq + i, h)),
        compiler_params=_params("parallel", "parallel", "arbitrary"),
        name="sb_prompt",
    )(bias, q, k, v)


def _sb_sample_kernel(pt_ref, bias_ref, q_ref, kn_ref, vn_ref, *rest, n_pg, scale):
    k_refs = rest[:n_pg]
    v_refs = rest[n_pg:2 * n_pg]
    o_ref = rest[2 * n_pg]
    acc_ref, la_ref = rest[2 * n_pg + 1:]
    step = pl.program_id(1)
    qr = SAMPLE_QROWS
    rows = N_HEADS * qr
    later_mat = _later_matrix(PAGE_SIZE)
    bias = jnp.concatenate(
        [jnp.full((qr, 1), bias_ref[h], F32) for h in range(N_HEADS)], axis=0)
    q = (q_ref[...] * scale).astype(BF16)

    def pages(k_list, v_list, mask):
        n = len(k_list)
        hcols = [slice(h * HEAD_DIM, (h + 1) * HEAD_DIM) for h in range(N_HEADS)]
        hrows = [slice(h * qr, (h + 1) * qr) for h in range(N_HEADS)]
        page_lanes = [slice(i * PAGE_SIZE, (i + 1) * PAGE_SIZE) for i in range(n)]
        zs = []
        for h in range(N_HEADS):
            k_cat = jnp.concatenate([k_ref[:, hcols[h]].astype(BF16) for k_ref in k_list], axis=0)
            zs.append(_dot_nt(q[hrows[h]], k_cat))
        z = jnp.concatenate(zs, axis=0) + bias
        t = jnp.log1p(jnp.exp(-jnp.abs(z)))
        log_keep = -(jnp.maximum(z, 0.0) + t)
        log_beta = jnp.minimum(z, 0.0) - t
        if mask is not None:
            log_keep = jnp.where(mask, log_keep, 0.0)
        hi, lo = _split(jnp.concatenate([log_keep[:, pl_] for pl_ in page_lanes], axis=0))
        within = _dot(hi, later_mat) + _dot(lo, later_mat)
        la = la_ref[...]
        later = []
        for i in range(n):
            later.append(within[i * rows:(i + 1) * rows] + la)
            la = la + jnp.sum(log_keep[:, page_lanes[i]], axis=-1, keepdims=True)
        la_ref[...] = la
        w = jnp.exp(log_beta + jnp.concatenate(later, axis=1))
        if mask is not None:
            w = jnp.where(mask, w, 0.0)
        w = w.astype(BF16)
        for h in range(N_HEADS):
            v_cat = jnp.concatenate([v_ref[:, hcols[h]].astype(BF16) for v_ref in v_list], axis=0)
            acc_ref[hrows[h], :] += _dot(w[hrows[h]], v_cat)

    @pl.when(step == 0)
    def _():
        acc_ref[...] = jnp.zeros_like(acc_ref)
        la_ref[...] = jnp.zeros_like(la_ref)
        r = lax.broadcasted_iota(jnp.int32, (rows, PAGE_SIZE), 0) % qr
        c = lax.broadcasted_iota(jnp.int32, (rows, PAGE_SIZE), 1)
        pages([kn_ref], [vn_ref], c < r)

    @pl.when(step > 0)
    def _():
        pages(list(k_refs), list(v_refs), None)

    @pl.when(step == pl.num_programs(1) - 1)
    def _():
        o_ref[...] = acc_ref[...]


def _sb_sample(q, k_new, v_new, bias, cache_k, cache_v, page_table, page0):
    bd, n_pages = page_table.shape
    n_pg = SAMPLE_PAGES_PER_STEP
    assert n_pages % n_pg == 0
    n_steps = n_pages // n_pg
    width = N_HEADS * HEAD_DIM
    rows = N_HEADS * SAMPLE_QROWS

    def page_spec(i):
        def index(b, s, pt):
            logical = n_pages - 1 - (jnp.maximum(s - 1, 0) * n_pg + i)
            return (pt[b, logical] + page0, 0, 0)
        return pl.BlockSpec((None, PAGE_SIZE, width), index)

    new_spec = pl.BlockSpec((None, PAGE_SIZE, width), lambda b, s, pt: (b, 0, 0))
    q_spec = pl.BlockSpec((None, rows, HEAD_DIM), lambda b, s, pt: (b, 0, 0))
    grid_spec = pltpu.PrefetchScalarGridSpec(
        num_scalar_prefetch=1,
        grid=(bd, n_steps + 1),
        in_specs=[pl.BlockSpec(memory_space=pltpu.SMEM), q_spec, new_spec, new_spec]
        + [page_spec(i) for i in range(n_pg)] * 2,
        out_specs=q_spec,
        scratch_shapes=[pltpu.VMEM((rows, HEAD_DIM), F32), pltpu.VMEM((rows, 1), F32)],
    )
    return pl.pallas_call(
        functools.partial(_sb_sample_kernel, n_pg=n_pg, scale=HEAD_DIM ** -0.5),
        out_shape=jax.ShapeDtypeStruct((bd, rows, HEAD_DIM), F32),
        grid_spec=grid_spec,
        compiler_params=_params("parallel", "arbitrary"),
        name="sb_sample",
    )(page_table, bias, q, k_new, v_new, *([cache_k] * n_pg), *([cache_v] * n_pg))


def _dn_prep_kernel(xq_ref, xk_ref, xv_ref, pq_ref, pk_ref, pv_ref, wq_ref, wk_ref, wv_ref,
                    oq_ref, ok_ref, ov_ref, *, seq, valid):
    rows8 = lax.broadcasted_iota(jnp.int32, (SUBLANES, HEAD_DIM), 0)

    def conv_silu(x_ref, p_ref, w_ref):
        w = w_ref[...]
        x8 = x_ref[0:SUBLANES, :]
        p8 = p_ref[...]
        head = x8 * w[DN_CONV - 1:DN_CONV]
        for k in range(1, DN_CONV):
            shifted = jnp.where(rows8 < k, pltpu.roll(p8, k, 0), pltpu.roll(x8, k, 0))
            head = head + shifted * w[DN_CONV - 1 - k:DN_CONV - k]
        if seq == SUBLANES:
            acc = head
        else:
            x = x_ref[...]
            acc = x * w[DN_CONV - 1:DN_CONV]
            for k in range(1, DN_CONV):
                acc = acc + pltpu.roll(x, k, 0) * w[DN_CONV - 1 - k:DN_CONV - k]
            acc = jnp.concatenate([head, acc[SUBLANES:]], axis=0)
        y = acc * _sigmoid(acc)
        if valid < seq:
            r = lax.broadcasted_iota(jnp.int32, y.shape, 0)
            y = jnp.where(r < valid, y, 0.0)
        return y

    def l2n(y):
        return y * lax.rsqrt(jnp.sum(y * y, axis=-1, keepdims=True) + L2_EPS)

    oq_ref[...] = l2n(conv_silu(xq_ref, pq_ref, wq_ref)) * (HEAD_DIM ** -0.5)
    ok_ref[...] = l2n(conv_silu(xk_ref, pk_ref, wk_ref))
    ov_ref[...] = conv_silu(xv_ref, pv_ref, wv_ref)


def _dn_prep(x, prev, conv_w, *, n_seq, seq, valid, name):
    nh = N_HEADS
    xs = [pl.BlockSpec((seq, HEAD_DIM), lambda b, h, s=s: (b, s * nh + h)) for s in range(3)]
    ps = [pl.BlockSpec((None, SUBLANES, HEAD_DIM), lambda b, h, s=s: (b, 0, s * nh + h)) for s in range(3)]
    ws = [pl.BlockSpec((SUBLANES, HEAD_DIM), lambda b, h, s=s: (0, s * nh + h)) for s in range(3)]
    out = pl.BlockSpec((seq, HEAD_DIM), lambda b, h: (b, h))
    shape = jax.ShapeDtypeStruct((n_seq * seq, BRANCH_WIDTH), F32)
    return pl.pallas_call(
        functools.partial(_dn_prep_kernel, seq=seq, valid=valid),
        out_shape=(shape, shape, shape),
        grid=(n_seq, nh),
        in_specs=xs + ps + ws,
        out_specs=(out, out, out),
        compiler_params=_params("parallel", "parallel"),
        name=name,
    )(x, x, x, prev, prev, prev, conv_w, conv_w, conv_w)


def _mm3(x, y):
    (xh, xl), (yh, yl) = x, y
    return _dot(xh, yh) + (_dot(xh, yl) + _dot(xl, yh))


def _unit_lower_inverse_each(mats, c):
    r = lax.broadcasted_iota(jnp.int32, (c, c), 0)
    col = lax.broadcasted_iota(jnp.int32, (c, c), 1)
    eye = jnp.where(r == col, 1.0, 0.0)
    p = [eye - a for a in mats]
    a_s = [_split(a) for a in mats]
    power = [_mm3(s, s) for s in a_s]
    for _ in range(int(math.log2(c)) - 2):
        pw_s = [_split(x) for x in power]
        p_s = [_split(x) for x in p]
        power = [_mm3(s, s) for s in pw_s]
        upd = [_mm3(ps, ws) for ps, ws in zip(p_s, pw_s)]
        p = [x + y for x, y in zip(p, upd)]
    pw_s = [_split(x) for x in power]
    p_s = [_split(x) for x in p]
    return [x + _mm3(ps, ws) for x, ps, ws in zip(p, p_s, pw_s)]


def _dn_chunk_kernel(ascale_ref, dtb_ref, q_ref, k_ref, v_ref, z_ref, ba_ref, bat_ref, nw_ref, s0_ref,
                     o_ref, s_ref, *, chunk, n_chunks):
    c = chunk
    heads = range(N_HEADS)

    @pl.when(pl.program_id(1) == 0)
    def _():
        s_ref[...] = s0_ref[...]

    r = lax.broadcasted_iota(jnp.int32, (c, c), 0)
    col = lax.broadcasted_iota(jnp.int32, (c, c), 1)
    tri = r >= col
    strict = r > col
    nw = nw_ref[...]

    def one_chunk(ci, carry):
        rows = pl.ds(pl.multiple_of(ci * c, c), c)
        bat = bat_ref[ci]
        hcols = [slice(h * HEAD_DIM, (h + 1) * HEAD_DIM) for h in heads]
        q = [q_ref[rows, hc] for hc in hcols]
        k = [k_ref[rows, hc] for hc in hcols]
        v = [v_ref[rows, hc] for hc in hcols]
        kb = [x.astype(BF16) for x in k]
        kk = [_dot_nt(x, x) for x in kb]
        qk = [_dot_nt(x.astype(BF16), y) for x, y in zip(q, kb)]
        beta, gc_col, g_last, decay, e_col = [], [], [], [], []
        for h in heads:
            a_scale = ascale_ref[h]
            beta.append(_sigmoid(ba_ref[rows, h:h + 1]))
            g_col = a_scale * _softplus(ba_ref[rows, N_HEADS + h:N_HEADS + h + 1] + dtb_ref[h])
            g_row = a_scale * _softplus(bat[N_HEADS + h:N_HEADS + h + 1, :] + dtb_ref[h])
            gcc = jnp.sum(jnp.where(tri, jnp.broadcast_to(g_row, (c, c)), 0.0), axis=1, keepdims=True)
            gcr = jnp.sum(jnp.where(r <= col, jnp.broadcast_to(g_col, (c, c)), 0.0), axis=0, keepdims=True)
            gc_col.append(gcc)
            g_last.append(jnp.sum(g_row, axis=1, keepdims=True))
            decay.append(jnp.exp(jnp.where(tri, gcc - gcr, -1e30)))
            e_col.append(jnp.exp(gcc))
        a_low = [jnp.where(strict, kk[h] * decay[h], 0.0) * beta[h] for h in heads]
        t_inv = _unit_lower_inverse_each(a_low, c)
        rhs = [jnp.concatenate([v[h] * beta[h], k[h] * (beta[h] * e_col[h])], axis=1) for h in heads]
        t_s = [_split(x) for x in t_inv]
        rhs_s = [_split(x) for x in rhs]
        sol = [_mm3(t_s[h], rhs_s[h]) for h in heads]
        state = [s_ref[h] for h in heads]
        sb = [x.astype(BF16) for x in state]
        u = [sol[h][:, :HEAD_DIM] - _dot(sol[h][:, HEAD_DIM:].astype(BF16), sb[h]) for h in heads]
        o_state = [_dot((q[h] * e_col[h]).astype(BF16), sb[h]) for h in heads]
        ub = [x.astype(BF16) for x in u]
        o = [o_state[h] + _dot(jnp.where(tri, qk[h] * decay[h], 0.0).astype(BF16), ub[h]) for h in heads]
        k_end = [(k[h] * jnp.exp(g_last[h] - gc_col[h])).astype(BF16) for h in heads]
        s_new = [state[h] * jnp.exp(g_last[h]) + _dot_tn(k_end[h], ub[h]) for h in heads]
        for h in heads:
            s_ref[h] = s_new[h]
            on = o[h] * lax.rsqrt(jnp.mean(o[h] * o[h], axis=-1, keepdims=True) + RMS_EPS)
            zz = z_ref[rows, hcols[h]]
            o_ref[rows, hcols[h]] = (on * nw * (zz * _sigmoid(zz))).astype(o_ref.dtype)
        return carry

    lax.fori_loop(0, n_chunks, one_chunk, 0)


def _dn_chunks(q, k, v, z_arr, z_col0, ba, bat, a_scale, dt_bias, norm_w, s0, *, n_seq, seq, chunk, rows):
    nr = seq // rows
    ncb = rows // chunk
    w = BRANCH_WIDTH
    zb = z_col0 // w
    blk = pl.BlockSpec((rows, w), lambda b, r: (b * nr + r, 0))
    state = pl.BlockSpec((None, N_HEADS, HEAD_DIM, HEAD_DIM), lambda b, r: (b, 0, 0, 0))
    smem = pl.BlockSpec(memory_space=pltpu.SMEM)
    return pl.pallas_call(
        functools.partial(_dn_chunk_kernel, chunk=chunk, n_chunks=ncb),
        out_shape=(jax.ShapeDtypeStruct((n_seq * seq, w), BF16),
                   jax.ShapeDtypeStruct((n_seq, N_HEADS, HEAD_DIM, HEAD_DIM), F32)),
        grid=(n_seq, nr),
        in_specs=[smem, smem, blk, blk, blk,
                  pl.BlockSpec((rows, w), lambda b, r: (b * nr + r, zb)),
                  pl.BlockSpec((rows, LANES), lambda b, r: (b * nr + r, 0)),
                  pl.BlockSpec((ncb, 2 * N_HEADS, chunk), lambda b, r: (b * nr + r, 0, 0)),
                  pl.BlockSpec((1, HEAD_DIM), lambda b, r: (0, 0)),
                  state],
        out_specs=(blk, state),
        compiler_params=_params("parallel", "arbitrary"),
        name=f"dn_chunks_{chunk}",
    )(a_scale, dt_bias, q, k, v, z_arr, ba, bat, norm_w, s0)


def _merge_kernel(osb_ref, odn_ref, wsb_ref, wdn_ref, gsb_ref, gdn_ref, o_ref):
    a = _dot(osb_ref[...], wsb_ref[...])
    b = _dot(odn_ref[...], wdn_ref[...])
    o_ref[...] = (_sigmoid(gsb_ref[...]) * a + _sigmoid(gdn_ref[...]) * b).astype(o_ref.dtype)


def _merge(o_sb, o_dn, w_sb, w_dn, gates, *, tm, tn, name):
    n = o_sb.shape[0]
    w = BRANCH_WIDTH
    nj = D_MODEL // tn
    return pl.pallas_call(
        _merge_kernel,
        out_shape=jax.ShapeDtypeStruct((n, D_MODEL), BF16),
        grid=(n // tm, nj),
        in_specs=[pl.BlockSpec((tm, w), lambda i, j: (i, 0)),
                  pl.BlockSpec((tm, w), lambda i, j: (i, 0)),
                  pl.BlockSpec((w, tn), lambda i, j: (0, j)),
                  pl.BlockSpec((w, tn), lambda i, j: (0, j)),
                  pl.BlockSpec((tm, tn), lambda i, j: (i, j)),
                  pl.BlockSpec((tm, tn), lambda i, j: (i, j + nj))],
        out_specs=pl.BlockSpec((tm, tn), lambda i, j: (i, j)),
        compiler_params=_params("parallel", "arbitrary"),
        name=name,
    )(o_sb, o_dn, w_sb, w_dn, gates, gates)


def _out_ln_kernel(m_ref, w_ref, xn_ref, ga_ref, sc_ref, sh_ref, g_ref, b_ref, wr_ref, br_ref,
                   x1_ref, h_ref, lg_ref):
    y = _dot(m_ref[...], w_ref[...])
    x1 = _layer_norm(DEEPNORM_ALPHA * xn_ref[...] + ga_ref[...] * y, g_ref[...], b_ref[...])
    x1_ref[...] = x1
    h = x1 * (1.0 + sc_ref[...]) + sh_ref[...]
    h_ref[...] = h.astype(BF16)
    lg_ref[...] = _dot3(h, wr_ref[...]) + br_ref[...]


def _out_ln(merged, w_out, xn, mod, ln_g, ln_b, w_router, b_router, *, tm, name):
    n = merged.shape[0]
    row = lambda i: i
    blk = pl.BlockSpec((tm, D_MODEL), lambda i: (i, 0))
    vec = pl.BlockSpec((1, D_MODEL), lambda i: (0, 0))
    return pl.pallas_call(
        _out_ln_kernel,
        out_shape=(jax.ShapeDtypeStruct((n, D_MODEL), F32), jax.ShapeDtypeStruct((n, D_MODEL), BF16),
                   jax.ShapeDtypeStruct((n, LANES), F32)),
        grid=(n // tm,),
        in_specs=[blk, pl.BlockSpec((D_MODEL, D_MODEL), lambda i: (0, 0)), blk,
                  mod.spec(2, tm, row), mod.spec(4, tm, row), mod.spec(3, tm, row), vec, vec,
                  pl.BlockSpec((D_MODEL, LANES), lambda i: (0, 0)),
                  pl.BlockSpec((1, LANES), lambda i: (0, 0))],
        out_specs=(blk, blk, pl.BlockSpec((tm, LANES), lambda i: (i, 0))),
        compiler_params=_params("parallel"),
        name=name,
    )(merged, w_out, xn, mod.arr, mod.arr, mod.arr, ln_g, ln_b, w_router, b_router)


def _moe_up_kernel(be_ref, na_ref, x_ref, wg_ref, wu_ref, bg_ref, bu_ref, o_ref, wg_bf, wu_bf):
    m = pl.program_id(1)
    prev = be_ref[jnp.maximum(m - 1, 0)]
    new_weights = jnp.logical_or(m == 0, be_ref[m] != prev)

    @pl.when(m < na_ref[0])
    def _():
        @pl.when(new_weights)
        def _():
            wg_bf[...] = wg_ref[...].astype(BF16)
            wu_bf[...] = wu_ref[...].astype(BF16)

        x = x_ref[...]
        gate = jnp.minimum(_dot(x, wg_bf[...]) + bg_ref[...], SWIGLU_LIMIT)
        up = jnp.clip(_dot(x, wu_bf[...]) + bu_ref[...], -SWIGLU_LIMIT, SWIGLU_LIMIT)
        o_ref[...] = ((up + 1.0) * gate * _sigmoid(gate * SWIGLU_ALPHA)).astype(o_ref.dtype)

    @pl.when(m >= na_ref[0])
    def _():
        o_ref[...] = jnp.zeros_like(o_ref)


def _moe_down_kernel(be_ref, na_ref, a_ref, w_ref, b_ref, g_ref, o_ref, w_bf):
    m = pl.program_id(1)
    prev = be_ref[jnp.maximum(m - 1, 0)]
    new_weights = jnp.logical_or(m == 0, be_ref[m] != prev)

    @pl.when(m < na_ref[0])
    def _():
        @pl.when(new_weights)
        def _():
            w_bf[...] = w_ref[...].astype(BF16)

        y = _dot(a_ref[...], w_bf[...]) + b_ref[...]
        o_ref[...] = (y * g_ref[...]).astype(o_ref.dtype)

    @pl.when(m >= na_ref[0])
    def _():
        o_ref[...] = jnp.zeros_like(o_ref)


def _moe_experts(xs, block_e, n_active, row_gate, w_gu, b_gu, w_dn, b_dn):
    n_rows = xs.shape[0]
    nb = n_rows // MOE_TM
    nc = D_EXPERT // MOE_TH
    b_gu3 = b_gu.reshape(N_EXPERTS, 1, 2 * D_EXPERT)
    b_dn3 = b_dn.reshape(N_EXPERTS, 1, D_MODEL)
    act = pl.pallas_call(
        _moe_up_kernel,
        out_shape=jax.ShapeDtypeStruct((n_rows, D_EXPERT), BF16),
        grid_spec=pltpu.PrefetchScalarGridSpec(
            num_scalar_prefetch=2,
            grid=(nc, nb),
            in_specs=[pl.BlockSpec((MOE_TM, D_MODEL), lambda c, m, be, na: (m, 0)),
                      pl.BlockSpec((None, D_MODEL, MOE_TH), lambda c, m, be, na: (be[m], 0, c)),
                      pl.BlockSpec((None, D_MODEL, MOE_TH), lambda c, m, be, na: (be[m], 0, nc + c)),
                      pl.BlockSpec((None, 1, MOE_TH), lambda c, m, be, na: (be[m], 0, c)),
                      pl.BlockSpec((None, 1, MOE_TH), lambda c, m, be, na: (be[m], 0, nc + c))],
            out_specs=pl.BlockSpec((MOE_TM, MOE_TH), lambda c, m, be, na: (m, c)),
            scratch_shapes=[pltpu.VMEM((D_MODEL, MOE_TH), BF16), pltpu.VMEM((D_MODEL, MOE_TH), BF16)],
        ),
        compiler_params=_params("arbitrary", "arbitrary"),
        name="moe_up",
    )(block_e, n_active, xs, w_gu, w_gu, b_gu3, b_gu3)
    nj = D_MODEL // MOE_TN
    return pl.pallas_call(
        _moe_down_kernel,
        out_shape=jax.ShapeDtypeStruct((n_rows, D_MODEL), BF16),
        grid_spec=pltpu.PrefetchScalarGridSpec(
            num_scalar_prefetch=2,
            grid=(nj, nb),
            in_specs=[pl.BlockSpec((MOE_TM, D_EXPERT), lambda j, m, be, na: (m, 0)),
                      pl.BlockSpec((None, D_EXPERT, MOE_TN), lambda j, m, be, na: (be[m], 0, j)),
                      pl.BlockSpec((None, 1, MOE_TN), lambda j, m, be, na: (be[m], 0, j)),
                      pl.BlockSpec((MOE_TM, 1), lambda j, m, be, na: (m, 0))],
            out_specs=pl.BlockSpec((MOE_TM, MOE_TN), lambda j, m, be, na: (m, j)),
            scratch_shapes=[pltpu.VMEM((D_EXPERT, MOE_TN), BF16)],
        ),
        compiler_params=_params("arbitrary", "arbitrary"),
        name="moe_down",
    )(block_e, n_active, act, w_dn, b_dn3, row_gate)


def _route(logits, tm):
    n = logits.shape[0]
    n_assign = n * TOP_K
    n_blocks = -(-n_assign // tm) + N_EXPERTS
    top_logit, top_e = lax.top_k(logits, TOP_K)
    gates = jax.nn.softmax(top_logit, axis=-1).reshape(-1)
    flat_e = top_e.reshape(-1).astype(jnp.int32)
    onehot = (flat_e[:, None] == jnp.arange(N_EXPERTS, dtype=jnp.int32)[None, :]).astype(jnp.int32)
    running = jnp.cumsum(onehot, axis=0)
    counts = running[-1]
    rank = jnp.sum(running * onehot, axis=1) - 1
    padded = (counts + tm - 1) // tm * tm
    pad_end = jnp.cumsum(padded)
    pad_start = pad_end - padded
    start = jnp.cumsum(counts) - counts
    slot_of = pad_start[flat_e] + rank
    order = jnp.argsort(flat_e, stable=True).astype(jnp.int32)
    block_e = jnp.minimum(jnp.searchsorted(pad_end, jnp.arange(n_blocks, dtype=jnp.int32) * tm, side='right'),
                          N_EXPERTS - 1).astype(jnp.int32)
    slot = jnp.arange(n_blocks * tm, dtype=jnp.int32)
    slot_e = jnp.repeat(block_e, tm)
    slot_rank = slot - pad_start[slot_e]
    valid = (slot_rank >= 0) & (slot_rank < counts[slot_e])
    src = order[jnp.clip(start[slot_e] + slot_rank, 0, n_assign - 1)]
    slot_tok = jnp.where(valid, src // TOP_K, 0)
    row_gate = jnp.where(valid, gates[src], 0.0).astype(F32)
    n_active = (pad_end[-1] // tm).astype(jnp.int32).reshape(1)
    return slot_tok, row_gate.reshape(-1, 1), block_e, n_active, slot_of.reshape(n, TOP_K)


def _combine_ln_kernel(x1_ref, y_ref, ga_ref, g_ref, b_ref, o_ref):
    ffn = jnp.sum(y_ref[...].astype(F32), axis=0)
    o_ref[...] = _layer_norm(DEEPNORM_ALPHA * x1_ref[...] + ga_ref[...] * ffn, g_ref[...], b_ref[...])


def _combine_ln(x1, y4, row0, mod, ln_g, ln_b, *, tm, name):
    n = x1.shape[0]
    rb = row0 // tm
    row = lambda i: i
    blk = pl.BlockSpec((tm, D_MODEL), lambda i: (i, 0))
    vec = pl.BlockSpec((1, D_MODEL), lambda i: (0, 0))
    return pl.pallas_call(
        _combine_ln_kernel,
        out_shape=jax.ShapeDtypeStruct((n, D_MODEL), F32),
        grid=(n // tm,),
        in_specs=[blk, pl.BlockSpec((TOP_K, tm, D_MODEL), lambda i: (0, i + rb, 0)),
                  mod.spec(5, tm, row), vec, vec],
        out_specs=blk,
        compiler_params=_params("parallel"),
        name=name,
    )(x1, y4, mod.arr, ln_g, ln_b)


def _dense_front(x, mod, w, ln_in_g, ln_in_b, *, tm, tag):
    xn, h = _ln_mod(x, ln_in_g, ln_in_b, mod, tm=tm, name=f"ln_mod_{tag}")
    mm = functools.partial(_matmul, h, w["w_in"], tm=tm)
    q = mm(tn=1024, out_dtype=F32, n_out=BRANCH_WIDTH, col0=0, name=f"proj_q_{tag}")
    k = mm(tn=1024, out_dtype=F32, n_out=BRANCH_WIDTH, col0=BRANCH_WIDTH, name=f"proj_k_{tag}")
    v = mm(tn=1024, out_dtype=F32, n_out=BRANCH_WIDTH, col0=2 * BRANCH_WIDTH, name=f"proj_v_{tag}")
    dnz = mm(tn=1024, out_dtype=F32, n_out=4 * BRANCH_WIDTH, col0=3 * BRANCH_WIDTH, name=f"proj_dn_{tag}")
    gates = mm(tn=1024, out_dtype=F32, n_out=2 * D_MODEL, col0=7 * BRANCH_WIDTH, name=f"proj_gate_{tag}")
    ba = mm(tn=LANES, out_dtype=F32, n_out=LANES, col0=7 * BRANCH_WIDTH + 2 * D_MODEL, name=f"proj_ba_{tag}")
    return xn, q, k, v, dnz, gates, ba


def _dense_back(o_sb, o_dn, gates, xn, mod, w, *, tm, tm_out, tag):
    merged = _merge(o_sb, o_dn, w["w_branch_sb"], w["w_branch_dn"], gates, tm=tm, tn=1024, name=f"merge_{tag}")
    return _out_ln(merged, w["w_out"], xn, mod, w["ln1_g"], w["ln1_b"], w["w_router"], w["b_router"],
                   tm=tm_out, name=f"out_ln_{tag}")


def _logit_rows(ba, n_seq, seq, chunk):
    t = ba[:, :2 * N_HEADS].reshape(n_seq * seq // chunk, chunk, 2 * N_HEADS)
    return t.transpose(0, 2, 1)


def kernel(x_prompt, x_sample, cache_k, cache_v, state_dn, state_conv, page_table, c_prompt, c_sample,
           ln_in_g, ln_in_b, w_ada, b_ada, w_in, sb_bias, conv_w, dn_a_log, dn_dt_bias, dn_norm_w,
           w_branch_sb, w_branch_dn, w_out, ln1_g, ln1_b, w_router, b_router, w_gu, b_gu,
           w_dn, b_dn, ln2_g, ln2_b):
    assert w_ada.shape[0] == DEPTH == 1
    bp, sp, d = x_prompt.shape
    bs, ss, _ = x_sample.shape
    n_p, n_s = bp * sp, bs * ss
    l = 0

    w_in_l = w_in[l]
    w_in_r = jnp.concatenate(
        [w_in_l[:, :OFF_DN_B], w_in_l[:, OFF_GATE:], w_in_l[:, OFF_DN_B:OFF_GATE],
         jnp.zeros((d, LANES - 2 * N_HEADS), w_in.dtype)], axis=1).astype(BF16)
    w = {
        "w_in": w_in_r,
        "w_branch_sb": w_branch_sb[l].astype(BF16), "w_branch_dn": w_branch_dn[l].astype(BF16),
        "w_out": w_out[l].astype(BF16),
        "ln1_g": ln1_g[l].reshape(1, d), "ln1_b": ln1_b[l].reshape(1, d),
        "w_router": jnp.pad(w_router[l], ((0, 0), (0, LANES - N_EXPERTS))),
        "b_router": jnp.pad(b_router[l], (0, LANES - N_EXPERTS)).reshape(1, LANES),
    }
    ln2g, ln2b = ln2_g[l].reshape(1, d), ln2_b[l].reshape(1, d)
    conv_w8 = jnp.pad(conv_w[l], ((0, SUBLANES - DN_CONV), (0, 0)))
    norm_w = dn_norm_w[l].reshape(1, HEAD_DIM)
    a_scale = -jnp.exp(dn_a_log[l].astype(F32))

    c_all = jnp.concatenate([c_prompt, c_sample], axis=0)
    mod_all = _matmul(c_all, w_ada[l], tm=c_all.shape[0], tn=1024, out_dtype=F32, n_out=6 * d,
                      bias=b_ada[l].reshape(1, 6 * d), silu_in=True, name="ada")
    mod_p = _Mod(mod_all[:bp], sp, n_p)
    mod_s = _Mod(mod_all[bp:], ss, n_s)

    xn_p, q_p, k_p, v_p, dnz_p, gates_p, ba_p = _dense_front(
        x_prompt.reshape(n_p, d), mod_p, w, ln_in_g, ln_in_b, tm=min(1024, n_p), tag="p")
    o_sb_p = _sb_prompt(q_p, k_p, v_p, sb_bias[l], n_seq=bp, seq=sp)
    zero_prev = jnp.zeros((bp, SUBLANES, 3 * BRANCH_WIDTH), F32)
    dq, dk, dv = _dn_prep(dnz_p, zero_prev, conv_w8, n_seq=bp, seq=sp, valid=sp, name="dn_prep_p")
    o_dn_p, s_p = _dn_chunks(dq, dk, dv, dnz_p, 3 * BRANCH_WIDTH, ba_p, _logit_rows(ba_p, bp, sp, DN_CHUNK),
                             a_scale, dn_dt_bias[l], norm_w,
                             jnp.zeros((bp, N_HEADS, HEAD_DIM, HEAD_DIM), F32),
                             n_seq=bp, seq=sp, chunk=DN_CHUNK, rows=4 * DN_CHUNK)
    x1_p, h2_p, lg_p = _dense_back(o_sb_p, o_dn_p, gates_p, xn_p, mod_p, w, tm=min(1024, n_p), tm_out=256,
                                   tag="p")

    xn_s, q_s, k_s, v_s, dnz_s, gates_s, ba_s = _dense_front(
        x_sample.reshape(n_s, d), mod_s, w, ln_in_g, ln_in_b, tm=n_s, tag="s")
    pad_q = SUBLANES - ss

    def heads_rows(t):
        t = t.reshape(bs, ss, N_HEADS, HEAD_DIM).transpose(0, 2, 1, 3)
        return (jnp.pad(t, ((0, 0), (0, 0), (0, SAMPLE_QROWS - ss), (0, 0)))
                .reshape(bs, N_HEADS * SAMPLE_QROWS, HEAD_DIM))

    def new_page(t):
        return jnp.pad(t.reshape(bs, ss, BRANCH_WIDTH), ((0, 0), (0, PAGE_SIZE - ss), (0, 0)))

    n_phys = cache_k.shape[1]
    o_s = _sb_sample(heads_rows(q_s), new_page(k_s), new_page(v_s), sb_bias[l],
                     cache_k.reshape(-1, PAGE_SIZE, BRANCH_WIDTH),
                     cache_v.reshape(-1, PAGE_SIZE, BRANCH_WIDTH), page_table, l * n_phys)
    o_sb_s = (o_s.reshape(bs, N_HEADS, SAMPLE_QROWS, HEAD_DIM)[:, :, :ss].transpose(0, 2, 1, 3)
              .reshape(n_s, BRANCH_WIDTH).astype(BF16))

    def pad_rows(t):
        return jnp.pad(t.reshape(bs, ss, -1), ((0, 0), (0, pad_q), (0, 0))).reshape(bs * SUBLANES, -1)

    dnz_s8 = pad_rows(dnz_s)
    ba_s8 = pad_rows(ba_s)
    prev_s = jnp.pad(state_conv[l], ((0, 0), (SUBLANES - (DN_CONV - 1), 0), (0, 0)))
    dq, dk, dv = _dn_prep(dnz_s8, prev_s, conv_w8, n_seq=bs, seq=SUBLANES, valid=ss, name="dn_prep_s")
    tok = jnp.arange(bs * SUBLANES) % SUBLANES
    neutral = jnp.concatenate([jnp.full((N_HEADS,), -1e30, F32), jnp.full((N_HEADS,), -1e30, F32),
                               jnp.zeros((LANES - 2 * N_HEADS,), F32)])
    ba_s8 = jnp.where((tok < ss)[:, None], ba_s8, neutral[None, :])
    o_dn_s8, s_s = _dn_chunks(dq, dk, dv, dnz_s8, 3 * BRANCH_WIDTH, ba_s8, _logit_rows(ba_s8, bs, SUBLANES, SUBLANES),
                              a_scale, dn_dt_bias[l], norm_w, state_dn[l],
                              n_seq=bs, seq=SUBLANES, chunk=SUBLANES, rows=SUBLANES)
    o_dn_s = o_dn_s8.reshape(bs, SUBLANES, BRANCH_WIDTH)[:, :ss].reshape(n_s, BRANCH_WIDTH)
    x1_s, h2_s, lg_s = _dense_back(o_sb_s, o_dn_s, gates_s, xn_s, mod_s, w, tm=n_s, tm_out=n_s, tag="s")

    logits = jnp.concatenate([lg_p[:, :N_EXPERTS], lg_s[:, :N_EXPERTS]], axis=0)
    h2 = jnp.concatenate([h2_p, h2_s], axis=0)
    slot_tok, row_gate, block_e, n_active, slot_of = _route(logits, MOE_TM)
    xs = jnp.take(h2, slot_tok, axis=0)
    yb = _moe_experts(xs, block_e, n_active, row_gate, w_gu[l], b_gu[l], w_dn[l], b_dn[l])
    y4 = jnp.take(yb, slot_of.T, axis=0)
    y_p = _combine_ln(x1_p, y4, 0, mod_p, ln2g, ln2b, tm=256, name="combine_ln_p")
    y_s = _combine_ln(x1_s, y4, n_p, mod_s, ln2g, ln2b, tm=n_s, name="combine_ln_s")

    heads = lambda t, b, s: t.reshape(1, b, s, N_HEADS, HEAD_DIM)
    conv_p = dnz_p.reshape(bp, sp, 4 * BRANCH_WIDTH)[:, sp - (DN_CONV - 1):, :3 * BRANCH_WIDTH]
    xp_s = jnp.concatenate([state_conv[l], dnz_s.reshape(bs, ss, 4 * BRANCH_WIDTH)[:, :, :3 * BRANCH_WIDTH]], axis=1)
    conv_s = xp_s[:, -(DN_CONV - 1):]
    return (y_p.reshape(bp, sp, d), y_s.reshape(bs, ss, d),
            heads(k_p, bp, sp), heads(v_p, bp, sp), s_p[None], conv_p[None],
            heads(k_s, bs, ss), heads(v_s, bs, ss), s_s[None], conv_s[None])
```

```python
import functools
import math

import jax
import jax.numpy as jnp
from jax import lax
from jax.experimental import pallas as pl
from jax.experimental.pallas import tpu as pltpu

F32 = jnp.float32
BF16 = jnp.bfloat16

D_MODEL = 2048
HEAD_DIM = 128
N_HEADS = 8
BRANCH_WIDTH = N_HEADS * HEAD_DIM
PAGE_SIZE = 128
DN_CONV = 4
DN_CHUNK = 64
N_EXPERTS = 32
TOP_K = 4
D_EXPERT = D_MODEL
SWIGLU_LIMIT = 7.0
SWIGLU_ALPHA = 1.702
LN_EPS = 1e-5
RMS_EPS = 1e-6
L2_EPS = 1e-6
DEPTH = 1
DEEPNORM_ALPHA = (2.0 * DEPTH) ** 0.25
OFF_DN_QKV = 3 * BRANCH_WIDTH
OFF_DN_Z = OFF_DN_QKV + 3 * BRANCH_WIDTH
OFF_DN_B = OFF_DN_Z + BRANCH_WIDTH
OFF_GATE = OFF_DN_B + 2 * N_HEADS
N_IN = OFF_GATE + 2 * D_MODEL

LANES = 128
SUBLANES = 8
VMEM_LIMIT = 48 * 1024 * 1024
MOE_TM = 512
MOE_TH = 512
MOE_TN = 1024
ROUTE_BLOCK = 512
SB_PROMPT_TILE = 256
SB_PROMPT_HEADS = 2
SAMPLE_PAGES_PER_STEP = 4
SAMPLE_QROWS = 16


def _params(*sem):
    return pltpu.CompilerParams(dimension_semantics=sem, vmem_limit_bytes=VMEM_LIMIT)


def _dot(a, b):
    return jnp.dot(a, b, preferred_element_type=F32)


def _dot_nt(a, b):
    return lax.dot_general(a, b, (((1,), (1,)), ((), ())), preferred_element_type=F32)


def _dot_tn(a, b):
    return lax.dot_general(a, b, (((0,), (0,)), ((), ())), preferred_element_type=F32)


def _split(x):
    hi = x.astype(BF16)
    lo = (x - hi.astype(F32)).astype(BF16)
    return hi, lo


def _dot3(a, b):
    ah, al = _split(a)
    bh, bl = _split(b)
    return _dot(ah, bh) + (_dot(ah, bl) + _dot(al, bh))


def _sigmoid(x):
    return 1.0 / (1.0 + jnp.exp(-x))


def _softplus(x):
    return jnp.maximum(x, 0.0) + jnp.log1p(jnp.exp(-jnp.abs(x)))


def _layer_norm(x, g, b):
    mu = jnp.mean(x, axis=-1, keepdims=True)
    xc = x - mu
    var = jnp.mean(xc * xc, axis=-1, keepdims=True)
    return xc * lax.rsqrt(var + LN_EPS) * g + b


class _Mod:
    def __init__(self, mod, rows_per_seq, n_rows):
        n_seq = mod.shape[0]
        self.grouped = rows_per_seq % SUBLANES == 0 and rows_per_seq >= LANES
        self.rows_per_seq = rows_per_seq
        m6 = mod.reshape(n_seq, 6, D_MODEL).transpose(1, 0, 2)
        if self.grouped:
            self.arr = m6.reshape(6, n_seq, 1, D_MODEL)
        else:
            self.arr = jnp.repeat(m6, rows_per_seq, axis=1)
            assert self.arr.shape[1] == n_rows

    def spec(self, comp, tm, row_block_of):
        if self.grouped:
            bps = self.rows_per_seq // tm
            return pl.BlockSpec((None, None, 1, D_MODEL),
                                lambda *g: (comp, row_block_of(*g) // bps, 0, 0))
        return pl.BlockSpec((None, tm, D_MODEL), lambda *g: (comp, row_block_of(*g), 0))


def _mm_kernel(x_ref, w_ref, *rest, has_bias, silu_in):
    if has_bias:
        b_ref, o_ref = rest
    else:
        (o_ref,) = rest
    x = x_ref[...]
    if silu_in:
        x = x * _sigmoid(x)
    acc = _dot(x.astype(BF16), w_ref[...].astype(BF16))
    if has_bias:
        acc = acc + b_ref[...]
    o_ref[...] = acc.astype(o_ref.dtype)


def _matmul(x, w, *, tm, tn, out_dtype, n_out, col0=0, bias=None, silu_in=False, name):
    n, k = x.shape
    assert n % tm == 0 and n_out % tn == 0 and col0 % tn == 0
    cb = col0 // tn
    in_specs = [pl.BlockSpec((tm, k), lambda i, j: (i, 0)),
                pl.BlockSpec((k, tn), lambda i, j: (0, j + cb))]
    args = [x, w]
    if bias is not None:
        in_specs.append(pl.BlockSpec((1, tn), lambda i, j: (0, j + cb)))
        args.append(bias)
    return pl.pallas_call(
        functools.partial(_mm_kernel, has_bias=bias is not None, silu_in=silu_in),
        out_shape=jax.ShapeDtypeStruct((n, n_out), out_dtype),
        grid=(n // tm, n_out // tn),
        in_specs=in_specs,
        out_specs=pl.BlockSpec((tm, tn), lambda i, j: (i, j)),
        compiler_params=_params("parallel", "arbitrary"),
        name=name,
    )(*args)


def _ln_mod_kernel(x_ref, g_ref, b_ref, sc_ref, sh_ref, xn_ref, h_ref):
    xn = _layer_norm(x_ref[...], g_ref[...], b_ref[...])
    xn_ref[...] = xn
    h_ref[...] = (xn * (1.0 + sc_ref[...]) + sh_ref[...]).astype(BF16)


def _ln_mod(x, g, b, mod, *, tm, name):
    n = x.shape[0]
    row = lambda i: i
    vec = pl.BlockSpec((1, D_MODEL), lambda i: (0, 0))
    blk = pl.BlockSpec((tm, D_MODEL), lambda i: (i, 0))
    return pl.pallas_call(
        _ln_mod_kernel,
        out_shape=(jax.ShapeDtypeStruct((n, D_MODEL), F32), jax.ShapeDtypeStruct((n, D_MODEL), BF16)),
        grid=(n // tm,),
        in_specs=[blk, vec, vec, mod.spec(1, tm, row), mod.spec(0, tm, row)],
        out_specs=(blk, blk),
        compiler_params=_params("parallel"),
        name=name,
    )(x, g.reshape(1, D_MODEL), b.reshape(1, D_MODEL), mod.arr, mod.arr)


def _stick_blocks(qs, kbs, vbs, biases, log_afters, later_mat, mask):
    n = range(len(qs))
    zs = [_dot_nt(qs[i], kbs[i]) + biases[i] for i in n]
    log_keep, log_beta = [], []
    for z in zs:
        t = jnp.log1p(jnp.exp(-jnp.abs(z)))
        lk = -(jnp.maximum(z, 0.0) + t)
        log_keep.append(lk if mask is None else jnp.where(mask, lk, 0.0))
        log_beta.append(jnp.minimum(z, 0.0) - t)
    parts = [_split(lk) for lk in log_keep]
    within = [_dot(hi, later_mat) + _dot(lo, later_mat) for hi, lo in parts]
    ws = []
    for i in n:
        w = jnp.exp(log_beta[i] + (within[i] + log_afters[i]))
        ws.append((w if mask is None else jnp.where(mask, w, 0.0)).astype(BF16))
    outs = [_dot(ws[i], vbs[i]) for i in n]
    new_after = [log_afters[i] + jnp.sum(log_keep[i], axis=-1, keepdims=True) for i in n]
    return outs, new_after


def _later_matrix(tk):
    r = lax.broadcasted_iota(jnp.int32, (tk, tk), 0)
    c = lax.broadcasted_iota(jnp.int32, (tk, tk), 1)
    return (r > c).astype(BF16)


def _sb_prompt_kernel(bias_ref, q_ref, k_ref, v_ref, o_ref, *, t, hp, scale):
    hg = pl.program_id(1)
    qi = pl.program_id(2)
    heads = range(hp)
    hcols = [slice(i * HEAD_DIM, (i + 1) * HEAD_DIM) for i in heads]
    biases = [bias_ref[hg * hp + i] for i in heads]
    qs = [(q_ref[:, hc] * scale).astype(BF16) for hc in hcols]
    later_mat = _later_matrix(t)
    r = lax.broadcasted_iota(jnp.int32, (t, t), 0)
    c = lax.broadcasted_iota(jnp.int32, (t, t), 1)

    def kv(j):
        rows = pl.ds(pl.multiple_of(j * t, t), t)
        return ([k_ref[rows, hc].astype(BF16) for hc in hcols],
                [v_ref[rows, hc].astype(BF16) for hc in hcols])

    kbs, vbs = kv(qi)
    zero = [jnp.zeros((t, 1), F32) for _ in heads]
    accs, afters = _stick_blocks(qs, kbs, vbs, biases, zero, later_mat, c < r)

    def body(step, carry):
        accs, afters = carry
        kbs, vbs = kv(qi - 1 - step)
        outs, afters = _stick_blocks(qs, kbs, vbs, biases, list(afters), later_mat, None)
        return tuple(a + o for a, o in zip(accs, outs)), tuple(afters)

    accs, _ = lax.fori_loop(0, qi, body, (tuple(accs), tuple(afters)))
    for i in heads:
        o_ref[:, hcols[i]] = accs[i].astype(o_ref.dtype)


def _sb_prompt(q, k, v, bias, *, n_seq, seq, t=SB_PROMPT_TILE, hp=SB_PROMPT_HEADS):
    nq = seq // t
    w = hp * HEAD_DIM
    return pl.pallas_call(
        functools.partial(_sb_prompt_kernel, t=t, hp=hp, scale=HEAD_DIM ** -0.5),
        out_shape=jax.ShapeDtypeStruct(q.shape, BF16),
        grid=(n_seq, N_HEADS // hp, nq),
        in_specs=[pl.BlockSpec(memory_space=pltpu.SMEM),
                  pl.BlockSpec((t, w), lambda b, h, i: (b * nq + i, h)),
                  pl.BlockSpec((seq, w), lambda b, h, i: (b, h)),
                  pl.BlockSpec((seq, w), lambda b, h, i: (b, h))],
        out_specs=pl.BlockSpec((t, w), lambda b, h, i: (b * nq + i, h)),
        compiler_params=_params("parallel", "parallel", "arbitrary"),
        name="sb_prompt",
    )(bias, q, k, v)


def _sb_sample_kernel(pt_ref, bias_ref, q_ref, kn_ref, vn_ref, *rest, n_pg, scale):
    k_refs = rest[:n_pg]
    v_refs = rest[n_pg:2 * n_pg]
    o_ref = rest[2 * n_pg]
    acc_ref, la_ref = rest[2 * n_pg + 1:]
    step = pl.program_id(1)
    qr = SAMPLE_QROWS
    rows = N_HEADS * qr
    later_mat = _later_matrix(PAGE_SIZE)
    bias = jnp.concatenate(
        [jnp.full((qr, 1), bias_ref[h], F32) for h in range(N_HEADS)], axis=0)
    q = (q_ref[...] * scale).astype(BF16)

    def head_rows(page_ref, h):
        return page_ref[pl.ds(h, PAGE_SIZE, stride=N_HEADS), :].astype(BF16)

    def pages(k_list, v_list, mask):
        n = len(k_list)
        hrows = [slice(h * qr, (h + 1) * qr) for h in range(N_HEADS)]
        page_lanes = [slice(i * PAGE_SIZE, (i + 1) * PAGE_SIZE) for i in range(n)]
        zs = []
        for h in range(N_HEADS):
            k_cat = jnp.concatenate([head_rows(k_ref, h) for k_ref in k_list], axis=0)
            zs.append(_dot_nt(q[hrows[h]], k_cat))
        z = jnp.concatenate(zs, axis=0) + bias
        t = jnp.log1p(jnp.exp(-jnp.abs(z)))
        log_keep = -(jnp.maximum(z, 0.0) + t)
        log_beta = jnp.minimum(z, 0.0) - t
        if mask is not None:
            log_keep = jnp.where(mask, log_keep, 0.0)
        hi, lo = _split(jnp.concatenate([log_keep[:, pl_] for pl_ in page_lanes], axis=0))
        within = _dot(hi, later_mat) + _dot(lo, later_mat)
        la = la_ref[...]
        later = []
        for i in range(n):
            later.append(within[i * rows:(i + 1) * rows] + la)
            la = la + jnp.sum(log_keep[:, page_lanes[i]], axis=-1, keepdims=True)
        la_ref[...] = la
        w = jnp.exp(log_beta + jnp.concatenate(later, axis=1))
        if mask is not None:
            w = jnp.where(mask, w, 0.0)
        w = w.astype(BF16)
        for h in range(N_HEADS):
            v_cat = jnp.concatenate([head_rows(v_ref, h) for v_ref in v_list], axis=0)
            acc_ref[hrows[h], :] += _dot(w[hrows[h]], v_cat)

    @pl.when(step == 0)
    def _():
        acc_ref[...] = jnp.zeros_like(acc_ref)
        la_ref[...] = jnp.zeros_like(la_ref)
        r = lax.broadcasted_iota(jnp.int32, (rows, PAGE_SIZE), 0) % qr
        c = lax.broadcasted_iota(jnp.int32, (rows, PAGE_SIZE), 1)
        pages([kn_ref], [vn_ref], c < r)

    @pl.when(step > 0)
    def _():
        pages(list(k_refs), list(v_refs), None)

    @pl.when(step == pl.num_programs(1) - 1)
    def _():
        o_ref[...] = acc_ref[...]


def _sb_sample(q, k_new, v_new, bias, cache_k, cache_v, page_table, page0):
    bd, n_pages = page_table.shape
    n_pg = SAMPLE_PAGES_PER_STEP
    assert n_pages % n_pg == 0
    n_steps = n_pages // n_pg
    page_rows = PAGE_SIZE * N_HEADS
    rows = N_HEADS * SAMPLE_QROWS

    def page_spec(i):
        def index(b, s, pt):
            logical = n_pages - 1 - (jnp.maximum(s - 1, 0) * n_pg + i)
            return (pt[b, logical] + page0, 0, 0)
        return pl.BlockSpec((None, page_rows, HEAD_DIM), index)

    new_spec = pl.BlockSpec((None, page_rows, HEAD_DIM), lambda b, s, pt: (b, 0, 0))
    q_spec = pl.BlockSpec((None, rows, HEAD_DIM), lambda b, s, pt: (b, 0, 0))
    grid_spec = pltpu.PrefetchScalarGridSpec(
        num_scalar_prefetch=1,
        grid=(bd, n_steps + 1),
        in_specs=[pl.BlockSpec(memory_space=pltpu.SMEM), q_spec, new_spec, new_spec]
        + [page_spec(i) for i in range(n_pg)] * 2,
        out_specs=q_spec,
        scratch_shapes=[pltpu.VMEM((rows, HEAD_DIM), F32), pltpu.VMEM((rows, 1), F32)],
    )
    return pl.pallas_call(
        functools.partial(_sb_sample_kernel, n_pg=n_pg, scale=HEAD_DIM ** -0.5),
        out_shape=jax.ShapeDtypeStruct((bd, rows, HEAD_DIM), F32),
        grid_spec=grid_spec,
        compiler_params=_params("parallel", "arbitrary"),
        name="sb_sample",
    )(page_table, bias, q, k_new, v_new, *([cache_k] * n_pg), *([cache_v] * n_pg))


def _dn_prep_kernel(xq_ref, xk_ref, xv_ref, pq_ref, pk_ref, pv_ref, wq_ref, wk_ref, wv_ref,
                    oq_ref, ok_ref, ov_ref, *, seq, valid):
    rows8 = lax.broadcasted_iota(jnp.int32, (SUBLANES, HEAD_DIM), 0)

    def conv_silu(x_ref, p_ref, w_ref):
        w = w_ref[...]
        x8 = x_ref[0:SUBLANES, :]
        p8 = p_ref[...]
        head = x8 * w[DN_CONV - 1:DN_CONV]
        for k in range(1, DN_CONV):
            shifted = jnp.where(rows8 < k, pltpu.roll(p8, k, 0), pltpu.roll(x8, k, 0))
            head = head + shifted * w[DN_CONV - 1 - k:DN_CONV - k]
        if seq == SUBLANES:
            acc = head
        else:
            x = x_ref[...]
            acc = x * w[DN_CONV - 1:DN_CONV]
            for k in range(1, DN_CONV):
                acc = acc + pltpu.roll(x, k, 0) * w[DN_CONV - 1 - k:DN_CONV - k]
            acc = jnp.concatenate([head, acc[SUBLANES:]], axis=0)
        y = acc * _sigmoid(acc)
        if valid < seq:
            r = lax.broadcasted_iota(jnp.int32, y.shape, 0)
            y = jnp.where(r < valid, y, 0.0)
        return y

    def l2n(y):
        return y * lax.rsqrt(jnp.sum(y * y, axis=-1, keepdims=True) + L2_EPS)

    oq_ref[...] = l2n(conv_silu(xq_ref, pq_ref, wq_ref)) * (HEAD_DIM ** -0.5)
    ok_ref[...] = l2n(conv_silu(xk_ref, pk_ref, wk_ref))
    ov_ref[...] = conv_silu(xv_ref, pv_ref, wv_ref)


def _dn_prep(x, prev, conv_w, *, n_seq, seq, valid, name):
    nh = N_HEADS
    xs = [pl.BlockSpec((seq, HEAD_DIM), lambda b, h, s=s: (b, s * nh + h)) for s in range(3)]
    ps = [pl.BlockSpec((None, SUBLANES, HEAD_DIM), lambda b, h, s=s: (b, 0, s * nh + h)) for s in range(3)]
    ws = [pl.BlockSpec((SUBLANES, HEAD_DIM), lambda b, h, s=s: (0, s * nh + h)) for s in range(3)]
    out = pl.BlockSpec((seq, HEAD_DIM), lambda b, h: (b, h))
    shape = jax.ShapeDtypeStruct((n_seq * seq, BRANCH_WIDTH), F32)
    return pl.pallas_call(
        functools.partial(_dn_prep_kernel, seq=seq, valid=valid),
        out_shape=(shape, shape, shape),
        grid=(n_seq, nh),
        in_specs=xs + ps + ws,
        out_specs=(out, out, out),
        compiler_params=_params("parallel", "parallel"),
        name=name,
    )(x, x, x, prev, prev, prev, conv_w, conv_w, conv_w)


def _mm3(x, y):
    (xh, xl), (yh, yl) = x, y
    return _dot(xh, yh) + (_dot(xh, yl) + _dot(xl, yh))


def _unit_lower_inverse_each(mats, c):
    r = lax.broadcasted_iota(jnp.int32, (c, c), 0)
    col = lax.broadcasted_iota(jnp.int32, (c, c), 1)
    eye = jnp.where(r == col, 1.0, 0.0)
    p = [eye - a for a in mats]
    a_s = [_split(a) for a in mats]
    power = [_mm3(s, s) for s in a_s]
    for _ in range(int(math.log2(c)) - 2):
        pw_s = [_split(x) for x in power]
        p_s = [_split(x) for x in p]
        power = [_mm3(s, s) for s in pw_s]
        upd = [_mm3(ps, ws) for ps, ws in zip(p_s, pw_s)]
        p = [x + y for x, y in zip(p, upd)]
    pw_s = [_split(x) for x in power]
    p_s = [_split(x) for x in p]
    return [x + _mm3(ps, ws) for x, ps, ws in zip(p, p_s, pw_s)]


def _dn_chunk_kernel(ascale_ref, dtb_ref, q_ref, k_ref, v_ref, z_ref, ba_ref, bat_ref, nw_ref, s0_ref,
                     o_ref, s_ref, *, chunk, n_chunks):
    c = chunk
    heads = range(N_HEADS)

    @pl.when(pl.program_id(1) == 0)
    def _():
        s_ref[...] = s0_ref[...]

    r = lax.broadcasted_iota(jnp.int32, (c, c), 0)
    col = lax.broadcasted_iota(jnp.int32, (c, c), 1)
    tri = r >= col
    strict = r > col
    nw = nw_ref[...]

    def one_chunk(ci, carry):
        rows = pl.ds(pl.multiple_of(ci * c, c), c)
        bat = bat_ref[ci]
        hcols = [slice(h * HEAD_DIM, (h + 1) * HEAD_DIM) for h in heads]
        q = [q_ref[rows, hc] for hc in hcols]
        k = [k_ref[rows, hc] for hc in hcols]
        v = [v_ref[rows, hc] for hc in hcols]
        kb = [x.astype(BF16) for x in k]
        kk = [_dot_nt(x, x) for x in kb]
        qk = [_dot_nt(x.astype(BF16), y) for x, y in zip(q, kb)]
        beta, gc_col, g_last, decay, e_col = [], [], [], [], []
        for h in heads:
            a_scale = ascale_ref[h]
            beta.append(_sigmoid(ba_ref[rows, h:h + 1]))
            g_col = a_scale * _softplus(ba_ref[rows, N_HEADS + h:N_HEADS + h + 1] + dtb_ref[h])
            g_row = a_scale * _softplus(bat[N_HEADS + h:N_HEADS + h + 1, :] + dtb_ref[h])
            gcc = jnp.sum(jnp.where(tri, jnp.broadcast_to(g_row, (c, c)), 0.0), axis=1, keepdims=True)
            gcr = jnp.sum(jnp.where(r <= col, jnp.broadcast_to(g_col, (c, c)), 0.0), axis=0, keepdims=True)
            gc_col.append(gcc)
            g_last.append(jnp.sum(g_row, axis=1, keepdims=True))
            decay.append(jnp.exp(jnp.where(tri, gcc - gcr, -1e30)))
            e_col.append(jnp.exp(gcc))
        a_low = [jnp.where(strict, kk[h] * decay[h], 0.0) * beta[h] for h in heads]
        t_inv = _unit_lower_inverse_each(a_low, c)
        rhs = [jnp.concatenate([v[h] * beta[h], k[h] * (beta[h] * e_col[h])], axis=1) for h in heads]
        t_s = [_split(x) for x in t_inv]
        rhs_s = [_split(x) for x in rhs]
        sol = [_mm3(t_s[h], rhs_s[h]) for h in heads]
        state = [s_ref[h] for h in heads]
        sb = [x.astype(BF16) for x in state]
        u = [sol[h][:, :HEAD_DIM] - _dot(sol[h][:, HEAD_DIM:].astype(BF16), sb[h]) for h in heads]
        o_state = [_dot((q[h] * e_col[h]).astype(BF16), sb[h]) for h in heads]
        ub = [x.astype(BF16) for x in u]
        o = [o_state[h] + _dot(jnp.where(tri, qk[h] * decay[h], 0.0).astype(BF16), ub[h]) for h in heads]
        k_end = [(k[h] * jnp.exp(g_last[h] - gc_col[h])).astype(BF16) for h in heads]
        s_new = [state[h] * jnp.exp(g_last[h]) + _dot_tn(k_end[h], ub[h]) for h in heads]
        for h in heads:
            s_ref[h] = s_new[h]
            on = o[h] * lax.rsqrt(jnp.mean(o[h] * o[h], axis=-1, keepdims=True) + RMS_EPS)
            zz = z_ref[rows, hcols[h]]
            o_ref[rows, hcols[h]] = (on * nw * (zz * _sigmoid(zz))).astype(o_ref.dtype)
        return carry

    lax.fori_loop(0, n_chunks, one_chunk, 0)


def _dn_chunks(q, k, v, z_arr, z_col0, ba, bat, a_scale, dt_bias, norm_w, s0, *, n_seq, seq, chunk, rows):
    nr = seq // rows
    ncb = rows // chunk
    w = BRANCH_WIDTH
    zb = z_col0 // w
    blk = pl.BlockSpec((rows, w), lambda b, r: (b * nr + r, 0))
    state = pl.BlockSpec((None, N_HEADS, HEAD_DIM, HEAD_DIM), lambda b, r: (b, 0, 0, 0))
    smem = pl.BlockSpec(memory_space=pltpu.SMEM)
    return pl.pallas_call(
        functools.partial(_dn_chunk_kernel, chunk=chunk, n_chunks=ncb),
        out_shape=(jax.ShapeDtypeStruct((n_seq * seq, w), BF16),
                   jax.ShapeDtypeStruct((n_seq, N_HEADS, HEAD_DIM, HEAD_DIM), F32)),
        grid=(n_seq, nr),
        in_specs=[smem, smem, blk, blk, blk,
                  pl.BlockSpec((rows, w), lambda b, r: (b * nr + r, zb)),
                  pl.BlockSpec((rows, LANES), lambda b, r: (b * nr + r, 0)),
                  pl.BlockSpec((ncb, 2 * N_HEADS, chunk), lambda b, r: (b * nr + r, 0, 0)),
                  pl.BlockSpec((1, HEAD_DIM), lambda b, r: (0, 0)),
                  state],
        out_specs=(blk, state),
        compiler_params=_params("parallel", "arbitrary"),
        name=f"dn_chunks_{chunk}",
    )(a_scale, dt_bias, q, k, v, z_arr, ba, bat, norm_w, s0)


def _merge_kernel(osb_ref, odn_ref, wsb_ref, wdn_ref, gsb_ref, gdn_ref, o_ref):
    a = _dot(osb_ref[...], wsb_ref[...])
    b = _dot(odn_ref[...], wdn_ref[...])
    o_ref[...] = (_sigmoid(gsb_ref[...]) * a + _sigmoid(gdn_ref[...]) * b).astype(o_ref.dtype)


def _merge(o_sb, o_dn, w_sb, w_dn, gates, *, tm, tn, name):
    n = o_sb.shape[0]
    w = BRANCH_WIDTH
    nj = D_MODEL // tn
    return pl.pallas_call(
        _merge_kernel,
        out_shape=jax.ShapeDtypeStruct((n, D_MODEL), BF16),
        grid=(n // tm, nj),
        in_specs=[pl.BlockSpec((tm, w), lambda i, j: (i, 0)),
                  pl.BlockSpec((tm, w), lambda i, j: (i, 0)),
                  pl.BlockSpec((w, tn), lambda i, j: (0, j)),
                  pl.BlockSpec((w, tn), lambda i, j: (0, j)),
                  pl.BlockSpec((tm, tn), lambda i, j: (i, j)),
                  pl.BlockSpec((tm, tn), lambda i, j: (i, j + nj))],
        out_specs=pl.BlockSpec((tm, tn), lambda i, j: (i, j)),
        compiler_params=_params("parallel", "arbitrary"),
        name=name,
    )(o_sb, o_dn, w_sb, w_dn, gates, gates)


def _out_ln_kernel(m_ref, w_ref, xn_ref, ga_ref, sc_ref, sh_ref, g_ref, b_ref, wr_ref, br_ref,
                   x1_ref, h_ref, lg_ref):
    y = _dot(m_ref[...], w_ref[...])
    x1 = _layer_norm(DEEPNORM_ALPHA * xn_ref[...] + ga_ref[...] * y, g_ref[...], b_ref[...])
    x1_ref[...] = x1
    h = x1 * (1.0 + sc_ref[...]) + sh_ref[...]
    h_ref[...] = h.astype(BF16)
    lg_ref[...] = _dot3(h, wr_ref[...]) + br_ref[...]


def _out_ln(merged, w_out, xn, mod, ln_g, ln_b, w_router, b_router, *, tm, name):
    n = merged.shape[0]
    row = lambda i: i
    blk = pl.BlockSpec((tm, D_MODEL), lambda i: (i, 0))
    vec = pl.BlockSpec((1, D_MODEL), lambda i: (0, 0))
    return pl.pallas_call(
        _out_ln_kernel,
        out_shape=(jax.ShapeDtypeStruct((n, D_MODEL), F32), jax.ShapeDtypeStruct((n, D_MODEL), BF16),
                   jax.ShapeDtypeStruct((n, LANES), F32)),
        grid=(n // tm,),
        in_specs=[blk, pl.BlockSpec((D_MODEL, D_MODEL), lambda i: (0, 0)), blk,
                  mod.spec(2, tm, row), mod.spec(4, tm, row), mod.spec(3, tm, row), vec, vec,
                  pl.BlockSpec((D_MODEL, LANES), lambda i: (0, 0)),
                  pl.BlockSpec((1, LANES), lambda i: (0, 0))],
        out_specs=(blk, blk, pl.BlockSpec((tm, LANES), lambda i: (i, 0))),
        compiler_params=_params("parallel"),
        name=name,
    )(merged, w_out, xn, mod.arr, mod.arr, mod.arr, ln_g, ln_b, w_router, b_router)


def _moe_up_kernel(be_ref, na_ref, x_ref, wg_ref, wu_ref, bg_ref, bu_ref, o_ref, wg_bf, wu_bf):
    m = pl.program_id(1)
    prev = be_ref[jnp.maximum(m - 1, 0)]
    new_weights = jnp.logical_or(m == 0, be_ref[m] != prev)

    @pl.when(m < na_ref[0])
    def _():
        @pl.when(new_weights)
        def _():
            wg_bf[...] = wg_ref[...].astype(BF16)
            wu_bf[...] = wu_ref[...].astype(BF16)

        x = x_ref[...]
        gate = jnp.minimum(_dot(x, wg_bf[...]) + bg_ref[...], SWIGLU_LIMIT)
        up = jnp.clip(_dot(x, wu_bf[...]) + bu_ref[...], -SWIGLU_LIMIT, SWIGLU_LIMIT)
        o_ref[...] = ((up + 1.0) * gate * _sigmoid(gate * SWIGLU_ALPHA)).astype(o_ref.dtype)

    @pl.when(m >= na_ref[0])
    def _():
        o_ref[...] = jnp.zeros_like(o_ref)


def _moe_down_kernel(be_ref, na_ref, a_ref, w_ref, b_ref, g_ref, o_ref, w_bf):
    m = pl.program_id(1)
    prev = be_ref[jnp.maximum(m - 1, 0)]
    new_weights = jnp.logical_or(m == 0, be_ref[m] != prev)

    @pl.when(m < na_ref[0])
    def _():
        @pl.when(new_weights)
        def _():
            w_bf[...] = w_ref[...].astype(BF16)

        y = _dot(a_ref[...], w_bf[...]) + b_ref[...]
        o_ref[...] = (y * g_ref[...]).astype(o_ref.dtype)

    @pl.when(m >= na_ref[0])
    def _():
        o_ref[...] = jnp.zeros_like(o_ref)


def _moe_experts(xs, block_e, n_active, row_gate, w_gu, b_gu, w_dn, b_dn):
    n_rows = xs.shape[0]
    nb = n_rows // MOE_TM
    nc = D_EXPERT // MOE_TH
    b_gu3 = b_gu.reshape(N_EXPERTS, 1, 2 * D_EXPERT)
    b_dn3 = b_dn.reshape(N_EXPERTS, 1, D_MODEL)
    act = pl.pallas_call(
        _moe_up_kernel,
        out_shape=jax.ShapeDtypeStruct((n_rows, D_EXPERT), BF16),
        grid_spec=pltpu.PrefetchScalarGridSpec(
            num_scalar_prefetch=2,
            grid=(nc, nb),
            in_specs=[pl.BlockSpec((MOE_TM, D_MODEL), lambda c, m, be, na: (m, 0)),
                      pl.BlockSpec((None, D_MODEL, MOE_TH), lambda c, m, be, na: (be[m], 0, c)),
                      pl.BlockSpec((None, D_MODEL, MOE_TH), lambda c, m, be, na: (be[m], 0, nc + c)),
                      pl.BlockSpec((None, 1, MOE_TH), lambda c, m, be, na: (be[m], 0, c)),
                      pl.BlockSpec((None, 1, MOE_TH), lambda c, m, be, na: (be[m], 0, nc + c))],
            out_specs=pl.BlockSpec((MOE_TM, MOE_TH), lambda c, m, be, na: (m, c)),
            scratch_shapes=[pltpu.VMEM((D_MODEL, MOE_TH), BF16), pltpu.VMEM((D_MODEL, MOE_TH), BF16)],
        ),
        compiler_params=_params("arbitrary", "arbitrary"),
        name="moe_up",
    )(block_e, n_active, xs, w_gu, w_gu, b_gu3, b_gu3)
    nj = D_MODEL // MOE_TN
    return pl.pallas_call(
        _moe_down_kernel,
        out_shape=jax.ShapeDtypeStruct((n_rows, D_MODEL), F32),
        grid_spec=pltpu.PrefetchScalarGridSpec(
            num_scalar_prefetch=2,
            grid=(nj, nb),
            in_specs=[pl.BlockSpec((MOE_TM, D_EXPERT), lambda j, m, be, na: (m, 0)),
                      pl.BlockSpec((None, D_EXPERT, MOE_TN), lambda j, m, be, na: (be[m], 0, j)),
                      pl.BlockSpec((None, 1, MOE_TN), lambda j, m, be, na: (be[m], 0, j)),
                      pl.BlockSpec((MOE_TM, 1), lambda j, m, be, na: (m, 0))],
            out_specs=pl.BlockSpec((MOE_TM, MOE_TN), lambda j, m, be, na: (m, j)),
            scratch_shapes=[pltpu.VMEM((D_EXPERT, MOE_TN), BF16)],
        ),
        compiler_params=_params("arbitrary", "arbitrary"),
        name="moe_down",
    )(block_e, n_active, act, w_dn, b_dn3, row_gate)


def _rank_kernel(e_ref, rank_ref, cnt_ref, run_ref, *, rb):
    @pl.when(pl.program_id(0) == 0)
    def _():
        run_ref[...] = jnp.zeros_like(run_ref)

    expert = lax.broadcasted_iota(jnp.int32, (N_EXPERTS, rb), 0)
    onehot = expert == e_ref[...]
    r = lax.broadcasted_iota(jnp.int32, (rb, rb), 0)
    c = lax.broadcasted_iota(jnp.int32, (rb, rb), 1)
    before = _dot(jnp.where(onehot, 1.0, 0.0).astype(BF16), (r < c).astype(BF16))
    run = run_ref[...]
    rank = jnp.sum(jnp.where(onehot, before + run, 0.0), axis=0, keepdims=True)
    rank_ref[...] = rank.astype(jnp.int32)
    run = run + jnp.sum(jnp.where(onehot, 1.0, 0.0), axis=1, keepdims=True)
    run_ref[...] = run
    cnt_ref[...] = jnp.broadcast_to(run, cnt_ref.shape)


def _expert_ranks(flat_e, rb=ROUTE_BLOCK):
    n = flat_e.shape[0]
    flat_e = jnp.pad(flat_e, (0, -n % rb), constant_values=N_EXPERTS)
    nb = flat_e.shape[0] // rb
    rank, cnt = pl.pallas_call(
        functools.partial(_rank_kernel, rb=rb),
        out_shape=(jax.ShapeDtypeStruct((nb, 1, rb), jnp.int32),
                   jax.ShapeDtypeStruct((N_EXPERTS, LANES), F32)),
        grid=(nb,),
        in_specs=[pl.BlockSpec((None, 1, rb), lambda i: (i, 0, 0))],
        out_specs=(pl.BlockSpec((None, 1, rb), lambda i: (i, 0, 0)),
                   pl.BlockSpec((N_EXPERTS, LANES), lambda i: (0, 0))),
        scratch_shapes=[pltpu.VMEM((N_EXPERTS, 1), F32)],
        compiler_params=_params("arbitrary"),
        name="expert_ranks",
    )(flat_e.reshape(nb, 1, rb))
    return rank.reshape(-1)[:n], cnt[:, 0].astype(jnp.int32)


def _route(logits, tm):
    n = logits.shape[0]
    n_assign = n * TOP_K
    n_blocks = -(-n_assign // tm) + N_EXPERTS
    top_logit, top_e = lax.top_k(logits, TOP_K)
    gates = jax.nn.softmax(top_logit, axis=-1).reshape(-1)
    flat_e = top_e.reshape(-1).astype(jnp.int32)
    rank, counts = _expert_ranks(flat_e)
    padded = (counts + tm - 1) // tm * tm
    pad_end = jnp.cumsum(padded)
    pad_start = pad_end - padded
    start = jnp.cumsum(counts) - counts
    onehot = flat_e[:, None] == jnp.arange(N_EXPERTS, dtype=jnp.int32)[None, :]
    slot_of = jnp.sum(jnp.where(onehot, pad_start[None, :], 0), axis=1) + rank
    order = jnp.argsort(flat_e, stable=True).astype(jnp.int32)
    block_e = jnp.minimum(jnp.searchsorted(pad_end, jnp.arange(n_blocks, dtype=jnp.int32) * tm, side='right'),
                          N_EXPERTS - 1).astype(jnp.int32)
    slot = jnp.arange(n_blocks * tm, dtype=jnp.int32)
    slot_e = jnp.repeat(block_e, tm)
    slot_rank = slot - pad_start[slot_e]
    valid = (slot_rank >= 0) & (slot_rank < counts[slot_e])
    src = order[jnp.clip(start[slot_e] + slot_rank, 0, n_assign - 1)]
    slot_tok = jnp.where(valid, src // TOP_K, 0)
    row_gate = jnp.where(valid, gates[src], 0.0).astype(F32)
    n_active = (pad_end[-1] // tm).astype(jnp.int32).reshape(1)
    return slot_tok, row_gate.reshape(-1, 1), block_e, n_active, slot_of.reshape(n, TOP_K)


def _combine_ln_kernel(idx_ref, nxt_ref, x1_ref, y_hbm, ga_ref, g_ref, b_ref, o_ref, buf, sem, *, tm):
    i = pl.program_id(0)
    n = pl.num_programs(0)
    rows = TOP_K * tm

    def row_copy(table_ref, r, slot):
        return pltpu.make_async_copy(y_hbm.at[pl.ds(table_ref[0, r], 1)], buf.at[slot, pl.ds(r, 1)],
                                     sem.at[slot])

    def fetch(table_ref, slot):
        def body(r, carry):
            row_copy(table_ref, r, slot).start()
            return carry
        lax.fori_loop(0, rows, body, 0, unroll=8)

    @pl.when(i == 0)
    def _():
        fetch(idx_ref, 0)

    @pl.when(i + 1 < n)
    def _():
        fetch(nxt_ref, (i + 1) % 2)

    slot = i % 2

    def wait_body(r, carry):
        row_copy(idx_ref, r, slot).wait()
        return carry
    lax.fori_loop(0, rows, wait_body, 0, unroll=8)

    ffn = buf[slot, 0:tm, :]
    for k in range(1, TOP_K):
        ffn = ffn + buf[slot, k * tm:(k + 1) * tm, :]
    o_ref[...] = _layer_norm(DEEPNORM_ALPHA * x1_ref[...] + ga_ref[...] * ffn, g_ref[...], b_ref[...])


def _combine_ln(x1, yb, slot_of, mod, ln_g, ln_b, *, tm, name):
    n = x1.shape[0]
    nb = n // tm
    rows = TOP_K * tm
    table = slot_of.reshape(nb, tm, TOP_K).transpose(0, 2, 1).reshape(nb, 1, rows).astype(jnp.int32)
    row = lambda i: i
    blk = pl.BlockSpec((tm, D_MODEL), lambda i: (i, 0))
    vec = pl.BlockSpec((1, D_MODEL), lambda i: (0, 0))
    return pl.pallas_call(
        functools.partial(_combine_ln_kernel, tm=tm),
        out_shape=jax.ShapeDtypeStruct((n, D_MODEL), F32),
        grid=(nb,),
        in_specs=[pl.BlockSpec((None, 1, rows), lambda i: (i, 0, 0), memory_space=pltpu.SMEM),
                  pl.BlockSpec((None, 1, rows), lambda i: (jnp.minimum(i + 1, nb - 1), 0, 0),
                               memory_space=pltpu.SMEM),
                  blk, pl.BlockSpec(memory_space=pl.ANY), mod.spec(5, tm, row), vec, vec],
        out_specs=blk,
        scratch_shapes=[pltpu.VMEM((2, rows, D_MODEL), F32), pltpu.SemaphoreType.DMA((2,))],
        compiler_params=_params("arbitrary"),
        name=name,
    )(table, table, x1, yb, mod.arr, ln_g, ln_b)


def _dense_front(x, mod, w, ln_in_g, ln_in_b, *, tm, tag):
    xn, h = _ln_mod(x, ln_in_g, ln_in_b, mod, tm=tm, name=f"ln_mod_{tag}")
    mm = functools.partial(_matmul, h, w["w_in"], tm=tm)
    q = mm(tn=1024, out_dtype=F32, n_out=BRANCH_WIDTH, col0=0, name=f"proj_q_{tag}")
    k = mm(tn=1024, out_dtype=F32, n_out=BRANCH_WIDTH, col0=BRANCH_WIDTH, name=f"proj_k_{tag}")
    v = mm(tn=1024, out_dtype=F32, n_out=BRANCH_WIDTH, col0=2 * BRANCH_WIDTH, name=f"proj_v_{tag}")
    dnz = mm(tn=1024, out_dtype=F32, n_out=4 * BRANCH_WIDTH, col0=3 * BRANCH_WIDTH, name=f"proj_dn_{tag}")
    gates = mm(tn=1024, out_dtype=F32, n_out=2 * D_MODEL, col0=7 * BRANCH_WIDTH, name=f"proj_gate_{tag}")
    ba = mm(tn=LANES, out_dtype=F32, n_out=LANES, col0=7 * BRANCH_WIDTH + 2 * D_MODEL, name=f"proj_ba_{tag}")
    return xn, q, k, v, dnz, gates, ba


def _dense_back(o_sb, o_dn, gates, xn, mod, w, *, tm, tm_out, tag):
    merged = _merge(o_sb, o_dn, w["w_branch_sb"], w["w_branch_dn"], gates, tm=tm, tn=1024, name=f"merge_{tag}")
    return _out_ln(merged, w["w_out"], xn, mod, w["ln1_g"], w["ln1_b"], w["w_router"], w["b_router"],
                   tm=tm_out, name=f"out_ln_{tag}")


def _logit_rows(ba, n_seq, seq, chunk):
    t = ba[:, :2 * N_HEADS].reshape(n_seq * seq // chunk, chunk, 2 * N_HEADS)
    return t.transpose(0, 2, 1)


def kernel(x_prompt, x_sample, cache_k, cache_v, state_dn, state_conv, page_table, c_prompt, c_sample,
           ln_in_g, ln_in_b, w_ada, b_ada, w_in, sb_bias, conv_w, dn_a_log, dn_dt_bias, dn_norm_w,
           w_branch_sb, w_branch_dn, w_out, ln1_g, ln1_b, w_router, b_router, w_gu, b_gu,
           w_dn, b_dn, ln2_g, ln2_b):
    assert w_ada.shape[0] == DEPTH == 1
    bp, sp, d = x_prompt.shape
    bs, ss, _ = x_sample.shape
    n_p, n_s = bp * sp, bs * ss
    l = 0

    w_in_l = w_in[l]
    w_in_r = jnp.concatenate(
        [w_in_l[:, :OFF_DN_B], w_in_l[:, OFF_GATE:], w_in_l[:, OFF_DN_B:OFF_GATE],
         jnp.zeros((d, LANES - 2 * N_HEADS), w_in.dtype)], axis=1).astype(BF16)
    w = {
        "w_in": w_in_r,
        "w_branch_sb": w_branch_sb[l].astype(BF16), "w_branch_dn": w_branch_dn[l].astype(BF16),
        "w_out": w_out[l].astype(BF16),
        "ln1_g": ln1_g[l].reshape(1, d), "ln1_b": ln1_b[l].reshape(1, d),
        "w_router": jnp.pad(w_router[l], ((0, 0), (0, LANES - N_EXPERTS))),
        "b_router": jnp.pad(b_router[l], (0, LANES - N_EXPERTS)).reshape(1, LANES),
    }
    ln2g, ln2b = ln2_g[l].reshape(1, d), ln2_b[l].reshape(1, d)
    conv_w8 = jnp.pad(conv_w[l], ((0, SUBLANES - DN_CONV), (0, 0)))
    norm_w = dn_norm_w[l].reshape(1, HEAD_DIM)
    a_scale = -jnp.exp(dn_a_log[l].astype(F32))

    c_all = jnp.concatenate([c_prompt, c_sample], axis=0)
    mod_all = _matmul(c_all, w_ada[l], tm=c_all.shape[0], tn=1024, out_dtype=F32, n_out=6 * d,
                      bias=b_ada[l].reshape(1, 6 * d), silu_in=True, name="ada")
    mod_p = _Mod(mod_all[:bp], sp, n_p)
    mod_s = _Mod(mod_all[bp:], ss, n_s)

    xn_p, q_p, k_p, v_p, dnz_p, gates_p, ba_p = _dense_front(
        x_prompt.reshape(n_p, d), mod_p, w, ln_in_g, ln_in_b, tm=min(1024, n_p), tag="p")
    o_sb_p = _sb_prompt(q_p, k_p, v_p, sb_bias[l], n_seq=bp, seq=sp)
    zero_prev = jnp.zeros((bp, SUBLANES, 3 * BRANCH_WIDTH), F32)
    dq, dk, dv = _dn_prep(dnz_p, zero_prev, conv_w8, n_seq=bp, seq=sp, valid=sp, name="dn_prep_p")
    o_dn_p, s_p = _dn_chunks(dq, dk, dv, dnz_p, 3 * BRANCH_WIDTH, ba_p, _logit_rows(ba_p, bp, sp, DN_CHUNK),
                             a_scale, dn_dt_bias[l], norm_w,
                             jnp.zeros((bp, N_HEADS, HEAD_DIM, HEAD_DIM), F32),
                             n_seq=bp, seq=sp, chunk=DN_CHUNK, rows=4 * DN_CHUNK)
    x1_p, h2_p, lg_p = _dense_back(o_sb_p, o_dn_p, gates_p, xn_p, mod_p, w, tm=min(1024, n_p), tm_out=256,
                                   tag="p")

    xn_s, q_s, k_s, v_s, dnz_s, gates_s, ba_s = _dense_front(
        x_sample.reshape(n_s, d), mod_s, w, ln_in_g, ln_in_b, tm=n_s, tag="s")
    pad_q = SUBLANES - ss

    def heads_rows(t):
        t = t.reshape(bs, ss, N_HEADS, HEAD_DIM).transpose(0, 2, 1, 3)
        return (jnp.pad(t, ((0, 0), (0, 0), (0, SAMPLE_QROWS - ss), (0, 0)))
                .reshape(bs, N_HEADS * SAMPLE_QROWS, HEAD_DIM))

    def new_page(t):
        t = jnp.pad(t.reshape(bs, ss, BRANCH_WIDTH), ((0, 0), (0, PAGE_SIZE - ss), (0, 0)))
        return t.reshape(bs, PAGE_SIZE * N_HEADS, HEAD_DIM)

    n_phys = cache_k.shape[1]
    o_s = _sb_sample(heads_rows(q_s), new_page(k_s), new_page(v_s), sb_bias[l],
                     cache_k.reshape(-1, PAGE_SIZE * N_HEADS, HEAD_DIM),
                     cache_v.reshape(-1, PAGE_SIZE * N_HEADS, HEAD_DIM), page_table, l * n_phys)
    o_sb_s = (o_s.reshape(bs, N_HEADS, SAMPLE_QROWS, HEAD_DIM)[:, :, :ss].transpose(0, 2, 1, 3)
              .reshape(n_s, BRANCH_WIDTH).astype(BF16))

    def pad_rows(t):
        return jnp.pad(t.reshape(bs, ss, -1), ((0, 0), (0, pad_q), (0, 0))).reshape(bs * SUBLANES, -1)

    dnz_s8 = pad_rows(dnz_s)
    ba_s8 = pad_rows(ba_s)
    prev_s = jnp.pad(state_conv[l], ((0, 0), (SUBLANES - (DN_CONV - 1), 0), (0, 0)))
    dq, dk, dv = _dn_prep(dnz_s8, prev_s, conv_w8, n_seq=bs, seq=SUBLANES, valid=ss, name="dn_prep_s")
    tok = jnp.arange(bs * SUBLANES) % SUBLANES
    neutral = jnp.concatenate([jnp.full((N_HEADS,), -1e30, F32), jnp.full((N_HEADS,), -1e30, F32),
                               jnp.zeros((LANES - 2 * N_HEADS,), F32)])
    ba_s8 = jnp.where((tok < ss)[:, None], ba_s8, neutral[None, :])
    o_dn_s8, s_s = _dn_chunks(dq, dk, dv, dnz_s8, 3 * BRANCH_WIDTH, ba_s8, _logit_rows(ba_s8, bs, SUBLANES, SUBLANES),
                              a_scale, dn_dt_bias[l], norm_w, state_dn[l],
                              n_seq=bs, seq=SUBLANES, chunk=SUBLANES, rows=SUBLANES)
    o_dn_s = o_dn_s8.reshape(bs, SUBLANES, BRANCH_WIDTH)[:, :ss].reshape(n_s, BRANCH_WIDTH)
    x1_s, h2_s, lg_s = _dense_back(o_sb_s, o_dn_s, gates_s, xn_s, mod_s, w, tm=n_s, tm_out=n_s, tag="s")

    logits = jnp.concatenate([lg_p[:, :N_EXPERTS], lg_s[:, :N_EXPERTS]], axis=0)
    h2 = jnp.concatenate([h2_p, h2_s], axis=0)
    slot_tok, row_gate, block_e, n_active, slot_of = _route(logits, MOE_TM)
    xs = jnp.take(h2, slot_tok, axis=0)
    yb = _moe_experts(xs, block_e, n_active, row_gate, w_gu[l], b_gu[l], w_dn[l], b_dn[l])
    y_p = _combine_ln(x1_p, yb, slot_of[:n_p], mod_p, ln2g, ln2b, tm=256, name="combine_ln_p")
    y_s = _combine_ln(x1_s, yb, slot_of[n_p:], mod_s, ln2g, ln2b, tm=n_s, name="combine_ln_s")

    heads = lambda t, b, s: t.reshape(1, b, s, N_HEADS, HEAD_DIM)
    conv_p = dnz_p.reshape(bp, sp, 4 * BRANCH_WIDTH)[:, sp - (DN_CONV - 1):, :3 * BRANCH_WIDTH]
    xp_s = jnp.concatenate([state_conv[l], dnz_s.reshape(bs, ss, 4 * BRANCH_WIDTH)[:, :, :3 * BRANCH_WIDTH]], axis=1)
    conv_s = xp_s[:, -(DN_CONV - 1):]
    return (y_p.reshape(bp, sp, d), y_s.reshape(bs, ss, d),
            heads(k_p, bp, sp), heads(v_p, bp, sp), s_p[None], conv_p[None],
            heads(k_s, bs, ss), heads(v_s, bs, ss), s_s[None], conv_s[None])
```

```python
import functools
import math

import jax
import jax.numpy as jnp
from jax import lax
from jax.experimental import pallas as pl
from jax.experimental.pallas import tpu as pltpu

F32 = jnp.float32
BF16 = jnp.bfloat16

D_MODEL = 2048
HEAD_DIM = 128
N_HEADS = 8
BRANCH_WIDTH = N_HEADS * HEAD_DIM
PAGE_SIZE = 128
DN_CONV = 4
DN_CHUNK = 64
N_EXPERTS = 32
TOP_K = 4
D_EXPERT = D_MODEL
SWIGLU_LIMIT = 7.0
SWIGLU_ALPHA = 1.702
LN_EPS = 1e-5
RMS_EPS = 1e-6
L2_EPS = 1e-6
DEPTH = 1
DEEPNORM_ALPHA = (2.0 * DEPTH) ** 0.25
OFF_DN_QKV = 3 * BRANCH_WIDTH
OFF_DN_Z = OFF_DN_QKV + 3 * BRANCH_WIDTH
OFF_DN_B = OFF_DN_Z + BRANCH_WIDTH
OFF_GATE = OFF_DN_B + 2 * N_HEADS
N_IN = OFF_GATE + 2 * D_MODEL

LANES = 128
SUBLANES = 8
VMEM_LIMIT = 48 * 1024 * 1024
MOE_TM = 512
MOE_TH = 512
MOE_TN = 1024
ROUTE_BLOCK = 512
SB_PROMPT_TILE = 256
SB_PROMPT_HEADS = 2
SAMPLE_PAGES_PER_STEP = 8
SAMPLE_QROWS = 16


def _params(*sem):
    return pltpu.CompilerParams(dimension_semantics=sem, vmem_limit_bytes=VMEM_LIMIT)


def _dot(a, b):
    return jnp.dot(a, b, preferred_element_type=F32)


def _dot_nt(a, b):
    return lax.dot_general(a, b, (((1,), (1,)), ((), ())), preferred_element_type=F32)


def _dot_tn(a, b):
    return lax.dot_general(a, b, (((0,), (0,)), ((), ())), preferred_element_type=F32)


def _split(x):
    hi = x.astype(BF16)
    lo = (x - hi.astype(F32)).astype(BF16)
    return hi, lo


def _dot3(a, b):
    ah, al = _split(a)
    bh, bl = _split(b)
    return _dot(ah, bh) + (_dot(ah, bl) + _dot(al, bh))


def _sigmoid(x):
    return 1.0 / (1.0 + jnp.exp(-x))


def _softplus(x):
    return jnp.maximum(x, 0.0) + jnp.log1p(jnp.exp(-jnp.abs(x)))


def _layer_norm(x, g, b):
    mu = jnp.mean(x, axis=-1, keepdims=True)
    xc = x - mu
    var = jnp.mean(xc * xc, axis=-1, keepdims=True)
    return xc * lax.rsqrt(var + LN_EPS) * g + b


class _Mod:
    def __init__(self, mod, rows_per_seq, n_rows):
        n_seq = mod.shape[0]
        self.grouped = rows_per_seq % SUBLANES == 0 and rows_per_seq >= LANES
        self.rows_per_seq = rows_per_seq
        m6 = mod.reshape(n_seq, 6, D_MODEL).transpose(1, 0, 2)
        if self.grouped:
            self.arr = m6.reshape(6, n_seq, 1, D_MODEL)
        else:
            self.arr = jnp.repeat(m6, rows_per_seq, axis=1)
            assert self.arr.shape[1] == n_rows

    def spec(self, comp, tm, row_block_of):
        if self.grouped:
            bps = self.rows_per_seq // tm
            return pl.BlockSpec((None, None, 1, D_MODEL),
                                lambda *g: (comp, row_block_of(*g) // bps, 0, 0))
        return pl.BlockSpec((None, tm, D_MODEL), lambda *g: (comp, row_block_of(*g), 0))


def _mm_kernel(x_ref, w_ref, *rest, has_bias, silu_in):
    if has_bias:
        b_ref, o_ref = rest
    else:
        (o_ref,) = rest
    x = x_ref[...]
    if silu_in:
        x = x * _sigmoid(x)
    acc = _dot(x.astype(BF16), w_ref[...].astype(BF16))
    if has_bias:
        acc = acc + b_ref[...]
    o_ref[...] = acc.astype(o_ref.dtype)


def _matmul(x, w, *, tm, tn, out_dtype, n_out, col0=0, bias=None, silu_in=False, name):
    n, k = x.shape
    assert n % tm == 0 and n_out % tn == 0 and col0 % tn == 0
    cb = col0 // tn
    in_specs = [pl.BlockSpec((tm, k), lambda i, j: (i, 0)),
                pl.BlockSpec((k, tn), lambda i, j: (0, j + cb))]
    args = [x, w]
    if bias is not None:
        in_specs.append(pl.BlockSpec((1, tn), lambda i, j: (0, j + cb)))
        args.append(bias)
    return pl.pallas_call(
        functools.partial(_mm_kernel, has_bias=bias is not None, silu_in=silu_in),
        out_shape=jax.ShapeDtypeStruct((n, n_out), out_dtype),
        grid=(n // tm, n_out // tn),
        in_specs=in_specs,
        out_specs=pl.BlockSpec((tm, tn), lambda i, j: (i, j)),
        compiler_params=_params("parallel", "arbitrary"),
        name=name,
    )(*args)


def _ln_mod_kernel(x_ref, g_ref, b_ref, sc_ref, sh_ref, xn_ref, h_ref):
    xn = _layer_norm(x_ref[...], g_ref[...], b_ref[...])
    xn_ref[...] = xn
    h_ref[...] = (xn * (1.0 + sc_ref[...]) + sh_ref[...]).astype(BF16)


def _ln_mod(x, g, b, mod, *, tm, name):
    n = x.shape[0]
    row = lambda i: i
    vec = pl.BlockSpec((1, D_MODEL), lambda i: (0, 0))
    blk = pl.BlockSpec((tm, D_MODEL), lambda i: (i, 0))
    return pl.pallas_call(
        _ln_mod_kernel,
        out_shape=(jax.ShapeDtypeStruct((n, D_MODEL), F32), jax.ShapeDtypeStruct((n, D_MODEL), BF16)),
        grid=(n // tm,),
        in_specs=[blk, vec, vec, mod.spec(1, tm, row), mod.spec(0, tm, row)],
        out_specs=(blk, blk),
        compiler_params=_params("parallel"),
        name=name,
    )(x, g.reshape(1, D_MODEL), b.reshape(1, D_MODEL), mod.arr, mod.arr)


def _stick_blocks(qs, kbs, vbs, biases, log_afters, later_mat, mask):
    n = range(len(qs))
    zs = [_dot_nt(qs[i], kbs[i]) + biases[i] for i in n]
    log_keep, log_beta = [], []
    for z in zs:
        t = jnp.log(1.0 + jnp.exp(-jnp.abs(z)))
        lk = -(jnp.maximum(z, 0.0) + t)
        log_keep.append(lk if mask is None else jnp.where(mask, lk, 0.0))
        log_beta.append(jnp.minimum(z, 0.0) - t)
    parts = [_split(lk) for lk in log_keep]
    within = [_dot(hi, later_mat) + _dot(lo, later_mat) for hi, lo in parts]
    ws = []
    for i in n:
        w = jnp.exp(log_beta[i] + (within[i] + log_afters[i]))
        ws.append((w if mask is None else jnp.where(mask, w, 0.0)).astype(BF16))
    outs = [_dot(ws[i], vbs[i]) for i in n]
    new_after = [log_afters[i] + jnp.sum(log_keep[i], axis=-1, keepdims=True) for i in n]
    return outs, new_after


def _later_matrix(tk):
    r = lax.broadcasted_iota(jnp.int32, (tk, tk), 0)
    c = lax.broadcasted_iota(jnp.int32, (tk, tk), 1)
    return (r > c).astype(BF16)


def _sb_prompt_kernel(bias_ref, q_ref, k_ref, v_ref, o_ref, *, t, hp, scale):
    hg = pl.program_id(1)
    qi = pl.program_id(2)
    heads = range(hp)
    hcols = [slice(i * HEAD_DIM, (i + 1) * HEAD_DIM) for i in heads]
    biases = [bias_ref[hg * hp + i] for i in heads]
    qs = [(q_ref[:, hc] * scale).astype(BF16) for hc in hcols]
    later_mat = _later_matrix(t)
    r = lax.broadcasted_iota(jnp.int32, (t, t), 0)
    c = lax.broadcasted_iota(jnp.int32, (t, t), 1)

    def kv(j):
        rows = pl.ds(pl.multiple_of(j * t, t), t)
        return ([k_ref[rows, hc].astype(BF16) for hc in hcols],
                [v_ref[rows, hc].astype(BF16) for hc in hcols])

    kbs, vbs = kv(qi)
    zero = [jnp.zeros((t, 1), F32) for _ in heads]
    accs, afters = _stick_blocks(qs, kbs, vbs, biases, zero, later_mat, c < r)

    def body(step, carry):
        accs, afters = carry
        kbs, vbs = kv(qi - 1 - step)
        outs, afters = _stick_blocks(qs, kbs, vbs, biases, list(afters), later_mat, None)
        return tuple(a + o for a, o in zip(accs, outs)), tuple(afters)

    accs, _ = lax.fori_loop(0, qi, body, (tuple(accs), tuple(afters)))
    for i in heads:
        o_ref[:, hcols[i]] = accs[i].astype(o_ref.dtype)


def _sb_prompt(q, k, v, bias, *, n_seq, seq, t=SB_PROMPT_TILE, hp=SB_PROMPT_HEADS):
    nq = seq // t
    w = hp * HEAD_DIM
    return pl.pallas_call(
        functools.partial(_sb_prompt_kernel, t=t, hp=hp, scale=HEAD_DIM ** -0.5),
        out_shape=jax.ShapeDtypeStruct(q.shape, BF16),
        grid=(n_seq, N_HEADS // hp, nq),
        in_specs=[pl.BlockSpec(memory_space=pltpu.SMEM),
                  pl.BlockSpec((t, w), lambda b, h, i: (b * nq + i, h)),
                  pl.BlockSpec((seq, w), lambda b, h, i: (b, h)),
                  pl.BlockSpec((seq, w), lambda b, h, i: (b, h))],
        out_specs=pl.BlockSpec((t, w), lambda b, h, i: (b * nq + i, h)),
        compiler_params=_params("parallel", "parallel", "arbitrary"),
        name="sb_prompt",
    )(bias, q, k, v)


def _sb_sample_kernel(pt_ref, bias_ref, q_ref, kn_ref, vn_ref, *rest, n_pg, scale):
    k_refs = rest[:n_pg]
    v_refs = rest[n_pg:2 * n_pg]
    o_ref = rest[2 * n_pg]
    acc_ref, la_ref = rest[2 * n_pg + 1:]
    step = pl.program_id(1)
    qr = SAMPLE_QROWS
    rows = N_HEADS * qr
    later_mat = _later_matrix(PAGE_SIZE)
    bias = jnp.concatenate(
        [jnp.full((qr, 1), bias_ref[h], F32) for h in range(N_HEADS)], axis=0)
    q = (q_ref[...] * scale).astype(BF16)

    def head_rows(page_ref, h):
        return page_ref[pl.ds(h, PAGE_SIZE, stride=N_HEADS), :].astype(BF16)

    def pages(k_list, v_list, mask):
        n = len(k_list)
        hrows = [slice(h * qr, (h + 1) * qr) for h in range(N_HEADS)]
        page_lanes = [slice(i * PAGE_SIZE, (i + 1) * PAGE_SIZE) for i in range(n)]
        zs = []
        for h in range(N_HEADS):
            k_cat = jnp.concatenate([head_rows(k_ref, h) for k_ref in k_list], axis=0)
            zs.append(_dot_nt(q[hrows[h]], k_cat))
        z = jnp.concatenate(zs, axis=0) + bias
        t = jnp.log(1.0 + jnp.exp(-jnp.abs(z)))
        log_keep = -(jnp.maximum(z, 0.0) + t)
        log_beta = jnp.minimum(z, 0.0) - t
        if mask is not None:
            log_keep = jnp.where(mask, log_keep, 0.0)
        hi, lo = _split(jnp.concatenate([log_keep[:, pl_] for pl_ in page_lanes], axis=0))
        within = _dot(hi, later_mat) + _dot(lo, later_mat)
        la = la_ref[...]
        later = []
        for i in range(n):
            later.append(within[i * rows:(i + 1) * rows] + la)
            la = la + jnp.sum(log_keep[:, page_lanes[i]], axis=-1, keepdims=True)
        la_ref[...] = la
        w = jnp.exp(log_beta + jnp.concatenate(later, axis=1))
        if mask is not None:
            w = jnp.where(mask, w, 0.0)
        w = w.astype(BF16)
        for h in range(N_HEADS):
            v_cat = jnp.concatenate([head_rows(v_ref, h) for v_ref in v_list], axis=0)
            acc_ref[hrows[h], :] += _dot(w[hrows[h]], v_cat)

    @pl.when(step == 0)
    def _():
        acc_ref[...] = jnp.zeros_like(acc_ref)
        la_ref[...] = jnp.zeros_like(la_ref)
        r = lax.broadcasted_iota(jnp.int32, (rows, PAGE_SIZE), 0) % qr
        c = lax.broadcasted_iota(jnp.int32, (rows, PAGE_SIZE), 1)
        pages([kn_ref], [vn_ref], c < r)

    @pl.when(step > 0)
    def _():
        pages(list(k_refs), list(v_refs), None)

    @pl.when(step == pl.num_programs(1) - 1)
    def _():
        o_ref[...] = acc_ref[...]


def _sb_sample(q, k_new, v_new, bias, cache_k, cache_v, page_table, page0):
    bd, n_pages = page_table.shape
    n_pg = SAMPLE_PAGES_PER_STEP
    assert n_pages % n_pg == 0
    n_steps = n_pages // n_pg
    page_rows = PAGE_SIZE * N_HEADS
    rows = N_HEADS * SAMPLE_QROWS

    def page_spec(i):
        def index(b, s, pt):
            logical = n_pages - 1 - (jnp.maximum(s - 1, 0) * n_pg + i)
            return (pt[b, logical] + page0, 0, 0)
        return pl.BlockSpec((None, page_rows, HEAD_DIM), index)

    new_spec = pl.BlockSpec((None, page_rows, HEAD_DIM), lambda b, s, pt: (b, 0, 0))
    q_spec = pl.BlockSpec((None, rows, HEAD_DIM), lambda b, s, pt: (b, 0, 0))
    grid_spec = pltpu.PrefetchScalarGridSpec(
        num_scalar_prefetch=1,
        grid=(bd, n_steps + 1),
        in_specs=[pl.BlockSpec(memory_space=pltpu.SMEM), q_spec, new_spec, new_spec]
        + [page_spec(i) for i in range(n_pg)] * 2,
        out_specs=q_spec,
        scratch_shapes=[pltpu.VMEM((rows, HEAD_DIM), F32), pltpu.VMEM((rows, 1), F32)],
    )
    return pl.pallas_call(
        functools.partial(_sb_sample_kernel, n_pg=n_pg, scale=HEAD_DIM ** -0.5),
        out_shape=jax.ShapeDtypeStruct((bd, rows, HEAD_DIM), F32),
        grid_spec=grid_spec,
        compiler_params=_params("parallel", "arbitrary"),
        name="sb_sample",
    )(page_table, bias, q, k_new, v_new, *([cache_k] * n_pg), *([cache_v] * n_pg))


def _dn_prep_kernel(xq_ref, xk_ref, xv_ref, pq_ref, pk_ref, pv_ref, wq_ref, wk_ref, wv_ref,
                    oq_ref, ok_ref, ov_ref, *, seq, valid):
    rows8 = lax.broadcasted_iota(jnp.int32, (SUBLANES, HEAD_DIM), 0)

    def conv_silu(x_ref, p_ref, w_ref):
        w = w_ref[...]
        x8 = x_ref[0:SUBLANES, :]
        p8 = p_ref[...]
        head = x8 * w[DN_CONV - 1:DN_CONV]
        for k in range(1, DN_CONV):
            shifted = jnp.where(rows8 < k, pltpu.roll(p8, k, 0), pltpu.roll(x8, k, 0))
            head = head + shifted * w[DN_CONV - 1 - k:DN_CONV - k]
        if seq == SUBLANES:
            acc = head
        else:
            x = x_ref[...]
            acc = x * w[DN_CONV - 1:DN_CONV]
            for k in range(1, DN_CONV):
                acc = acc + pltpu.roll(x, k, 0) * w[DN_CONV - 1 - k:DN_CONV - k]
            acc = jnp.concatenate([head, acc[SUBLANES:]], axis=0)
        y = acc * _sigmoid(acc)
        if valid < seq:
            r = lax.broadcasted_iota(jnp.int32, y.shape, 0)
            y = jnp.where(r < valid, y, 0.0)
        return y

    def l2n(y):
        return y * lax.rsqrt(jnp.sum(y * y, axis=-1, keepdims=True) + L2_EPS)

    oq_ref[...] = l2n(conv_silu(xq_ref, pq_ref, wq_ref)) * (HEAD_DIM ** -0.5)
    ok_ref[...] = l2n(conv_silu(xk_ref, pk_ref, wk_ref))
    ov_ref[...] = conv_silu(xv_ref, pv_ref, wv_ref)


def _dn_prep(x, prev, conv_w, *, n_seq, seq, valid, name):
    nh = N_HEADS
    xs = [pl.BlockSpec((seq, HEAD_DIM), lambda b, h, s=s: (b, s * nh + h)) for s in range(3)]
    ps = [pl.BlockSpec((None, SUBLANES, HEAD_DIM), lambda b, h, s=s: (b, 0, s * nh + h)) for s in range(3)]
    ws = [pl.BlockSpec((SUBLANES, HEAD_DIM), lambda b, h, s=s: (0, s * nh + h)) for s in range(3)]
    out = pl.BlockSpec((seq, HEAD_DIM), lambda b, h: (b, h))
    shape = jax.ShapeDtypeStruct((n_seq * seq, BRANCH_WIDTH), F32)
    return pl.pallas_call(
        functools.partial(_dn_prep_kernel, seq=seq, valid=valid),
        out_shape=(shape, shape, shape),
        grid=(n_seq, nh),
        in_specs=xs + ps + ws,
        out_specs=(out, out, out),
        compiler_params=_params("parallel", "parallel"),
        name=name,
    )(x, x, x, prev, prev, prev, conv_w, conv_w, conv_w)


def _mm3(x, y):
    (xh, xl), (yh, yl) = x, y
    return _dot(xh, yh) + (_dot(xh, yl) + _dot(xl, yh))


def _unit_lower_inverse_each(mats, c):
    r = lax.broadcasted_iota(jnp.int32, (c, c), 0)
    col = lax.broadcasted_iota(jnp.int32, (c, c), 1)
    eye = jnp.where(r == col, 1.0, 0.0)
    p = [eye - a for a in mats]
    a_s = [_split(a) for a in mats]
    power = [_mm3(s, s) for s in a_s]
    for _ in range(int(math.log2(c)) - 2):
        pw_s = [_split(x) for x in power]
        p_s = [_split(x) for x in p]
        power = [_mm3(s, s) for s in pw_s]
        upd = [_mm3(ps, ws) for ps, ws in zip(p_s, pw_s)]
        p = [x + y for x, y in zip(p, upd)]
    pw_s = [_split(x) for x in power]
    p_s = [_split(x) for x in p]
    return [x + _mm3(ps, ws) for x, ps, ws in zip(p, p_s, pw_s)]


def _dn_chunk_kernel(ascale_ref, dtb_ref, q_ref, k_ref, v_ref, z_ref, ba_ref, bat_ref, nw_ref, s0_ref,
                     o_ref, s_ref, *, chunk, n_chunks):
    c = chunk
    heads = range(N_HEADS)

    @pl.when(pl.program_id(1) == 0)
    def _():
        s_ref[...] = s0_ref[...]

    r = lax.broadcasted_iota(jnp.int32, (c, c), 0)
    col = lax.broadcasted_iota(jnp.int32, (c, c), 1)
    tri = r >= col
    strict = r > col
    nw = nw_ref[...]

    def one_chunk(ci, carry):
        rows = pl.ds(pl.multiple_of(ci * c, c), c)
        bat = bat_ref[ci]
        hcols = [slice(h * HEAD_DIM, (h + 1) * HEAD_DIM) for h in heads]
        q = [q_ref[rows, hc] for hc in hcols]
        k = [k_ref[rows, hc] for hc in hcols]
        v = [v_ref[rows, hc] for hc in hcols]
        kb = [x.astype(BF16) for x in k]
        kk = [_dot_nt(x, x) for x in kb]
        qk = [_dot_nt(x.astype(BF16), y) for x, y in zip(q, kb)]
        beta, gc_col, g_last, decay, e_col = [], [], [], [], []
        for h in heads:
            a_scale = ascale_ref[h]
            beta.append(_sigmoid(ba_ref[rows, h:h + 1]))
            g_col = a_scale * _softplus(ba_ref[rows, N_HEADS + h:N_HEADS + h + 1] + dtb_ref[h])
            g_row = a_scale * _softplus(bat[N_HEADS + h:N_HEADS + h + 1, :] + dtb_ref[h])
            gcc = jnp.sum(jnp.where(tri, jnp.broadcast_to(g_row, (c, c)), 0.0), axis=1, keepdims=True)
            gcr = jnp.sum(jnp.where(r <= col, jnp.broadcast_to(g_col, (c, c)), 0.0), axis=0, keepdims=True)
            gc_col.append(gcc)
            g_last.append(jnp.sum(g_row, axis=1, keepdims=True))
            decay.append(jnp.exp(jnp.where(tri, gcc - gcr, -1e30)))
            e_col.append(jnp.exp(gcc))
        a_low = [jnp.where(strict, kk[h] * decay[h], 0.0) * beta[h] for h in heads]
        t_inv = _unit_lower_inverse_each(a_low, c)
        rhs = [jnp.concatenate([v[h] * beta[h], k[h] * (beta[h] * e_col[h])], axis=1) for h in heads]
        t_s = [_split(x) for x in t_inv]
        rhs_s = [_split(x) for x in rhs]
        sol = [_mm3(t_s[h], rhs_s[h]) for h in heads]
        state = [s_ref[h] for h in heads]
        sb = [x.astype(BF16) for x in state]
        u = [sol[h][:, :HEAD_DIM] - _dot(sol[h][:, HEAD_DIM:].astype(BF16), sb[h]) for h in heads]
        o_state = [_dot((q[h] * e_col[h]).astype(BF16), sb[h]) for h in heads]
        ub = [x.astype(BF16) for x in u]
        o = [o_state[h] + _dot(jnp.where(tri, qk[h] * decay[h], 0.0).astype(BF16), ub[h]) for h in heads]
        k_end = [(k[h] * jnp.exp(g_last[h] - gc_col[h])).astype(BF16) for h in heads]
        s_new = [state[h] * jnp.exp(g_last[h]) + _dot_tn(k_end[h], ub[h]) for h in heads]
        for h in heads:
            s_ref[h] = s_new[h]
            on = o[h] * lax.rsqrt(jnp.mean(o[h] * o[h], axis=-1, keepdims=True) + RMS_EPS)
            zz = z_ref[rows, hcols[h]]
            o_ref[rows, hcols[h]] = (on * nw * (zz * _sigmoid(zz))).astype(o_ref.dtype)
        return carry

    lax.fori_loop(0, n_chunks, one_chunk, 0)


def _dn_chunks(q, k, v, z_arr, z_col0, ba, bat, a_scale, dt_bias, norm_w, s0, *, n_seq, seq, chunk, rows):
    nr = seq // rows
    ncb = rows // chunk
    w = BRANCH_WIDTH
    zb = z_col0 // w
    blk = pl.BlockSpec((rows, w), lambda b, r: (b * nr + r, 0))
    state = pl.BlockSpec((None, N_HEADS, HEAD_DIM, HEAD_DIM), lambda b, r: (b, 0, 0, 0))
    smem = pl.BlockSpec(memory_space=pltpu.SMEM)
    return pl.pallas_call(
        functools.partial(_dn_chunk_kernel, chunk=chunk, n_chunks=ncb),
        out_shape=(jax.ShapeDtypeStruct((n_seq * seq, w), BF16),
                   jax.ShapeDtypeStruct((n_seq, N_HEADS, HEAD_DIM, HEAD_DIM), F32)),
        grid=(n_seq, nr),
        in_specs=[smem, smem, blk, blk, blk,
                  pl.BlockSpec((rows, w), lambda b, r: (b * nr + r, zb)),
                  pl.BlockSpec((rows, LANES), lambda b, r: (b * nr + r, 0)),
                  pl.BlockSpec((ncb, 2 * N_HEADS, chunk), lambda b, r: (b * nr + r, 0, 0)),
                  pl.BlockSpec((1, HEAD_DIM), lambda b, r: (0, 0)),
                  state],
        out_specs=(blk, state),
        compiler_params=_params("parallel", "arbitrary"),
        name=f"dn_chunks_{chunk}",
    )(a_scale, dt_bias, q, k, v, z_arr, ba, bat, norm_w, s0)


def _merge_kernel(osb_ref, odn_ref, wsb_ref, wdn_ref, gsb_ref, gdn_ref, o_ref):
    a = _dot(osb_ref[...], wsb_ref[...])
    b = _dot(odn_ref[...], wdn_ref[...])
    o_ref[...] = (_sigmoid(gsb_ref[...]) * a + _sigmoid(gdn_ref[...]) * b).astype(o_ref.dtype)


def _merge(o_sb, o_dn, w_sb, w_dn, gates, *, tm, tn, name):
    n = o_sb.shape[0]
    w = BRANCH_WIDTH
    nj = D_MODEL // tn
    return pl.pallas_call(
        _merge_kernel,
        out_shape=jax.ShapeDtypeStruct((n, D_MODEL), BF16),
        grid=(n // tm, nj),
        in_specs=[pl.BlockSpec((tm, w), lambda i, j: (i, 0)),
                  pl.BlockSpec((tm, w), lambda i, j: (i, 0)),
                  pl.BlockSpec((w, tn), lambda i, j: (0, j)),
                  pl.BlockSpec((w, tn), lambda i, j: (0, j)),
                  pl.BlockSpec((tm, tn), lambda i, j: (i, j)),
                  pl.BlockSpec((tm, tn), lambda i, j: (i, j + nj))],
        out_specs=pl.BlockSpec((tm, tn), lambda i, j: (i, j)),
        compiler_params=_params("parallel", "arbitrary"),
        name=name,
    )(o_sb, o_dn, w_sb, w_dn, gates, gates)


def _out_ln_kernel(m_ref, w_ref, xn_ref, ga_ref, sc_ref, sh_ref, g_ref, b_ref, wr_ref, br_ref,
                   x1_ref, h_ref, lg_ref):
    y = _dot(m_ref[...], w_ref[...])
    x1 = _layer_norm(DEEPNORM_ALPHA * xn_ref[...] + ga_ref[...] * y, g_ref[...], b_ref[...])
    x1_ref[...] = x1
    h = x1 * (1.0 + sc_ref[...]) + sh_ref[...]
    h_ref[...] = h.astype(BF16)
    lg_ref[...] = _dot3(h, wr_ref[...]) + br_ref[...]


def _out_ln(merged, w_out, xn, mod, ln_g, ln_b, w_router, b_router, *, tm, name):
    n = merged.shape[0]
    row = lambda i: i
    blk = pl.BlockSpec((tm, D_MODEL), lambda i: (i, 0))
    vec = pl.BlockSpec((1, D_MODEL), lambda i: (0, 0))
    return pl.pallas_call(
        _out_ln_kernel,
        out_shape=(jax.ShapeDtypeStruct((n, D_MODEL), F32), jax.ShapeDtypeStruct((n, D_MODEL), BF16),
                   jax.ShapeDtypeStruct((n, LANES), F32)),
        grid=(n // tm,),
        in_specs=[blk, pl.BlockSpec((D_MODEL, D_MODEL), lambda i: (0, 0)), blk,
                  mod.spec(2, tm, row), mod.spec(4, tm, row), mod.spec(3, tm, row), vec, vec,
                  pl.BlockSpec((D_MODEL, LANES), lambda i: (0, 0)),
                  pl.BlockSpec((1, LANES), lambda i: (0, 0))],
        out_specs=(blk, blk, pl.BlockSpec((tm, LANES), lambda i: (i, 0))),
        compiler_params=_params("parallel"),
        name=name,
    )(merged, w_out, xn, mod.arr, mod.arr, mod.arr, ln_g, ln_b, w_router, b_router)


def _moe_up_kernel(be_ref, na_ref, x_ref, wg_ref, wu_ref, bg_ref, bu_ref, o_ref, wg_bf, wu_bf):
    m = pl.program_id(1)
    prev = be_ref[jnp.maximum(m - 1, 0)]
    new_weights = jnp.logical_or(m == 0, be_ref[m] != prev)

    @pl.when(m < na_ref[0])
    def _():
        @pl.when(new_weights)
        def _():
            wg_bf[...] = wg_ref[...].astype(BF16)
            wu_bf[...] = wu_ref[...].astype(BF16)

        x = x_ref[...]
        gate = jnp.minimum(_dot(x, wg_bf[...]) + bg_ref[...], SWIGLU_LIMIT)
        up = jnp.clip(_dot(x, wu_bf[...]) + bu_ref[...], -SWIGLU_LIMIT, SWIGLU_LIMIT)
        o_ref[...] = ((up + 1.0) * gate * _sigmoid(gate * SWIGLU_ALPHA)).astype(o_ref.dtype)

    @pl.when(m >= na_ref[0])
    def _():
        o_ref[...] = jnp.zeros_like(o_ref)


def _moe_down_kernel(be_ref, na_ref, a_ref, w_ref, b_ref, o_ref, w_bf):
    m = pl.program_id(1)
    prev = be_ref[jnp.maximum(m - 1, 0)]
    new_weights = jnp.logical_or(m == 0, be_ref[m] != prev)

    @pl.when(m < na_ref[0])
    def _():
        @pl.when(new_weights)
        def _():
            w_bf[...] = w_ref[...].astype(BF16)

        o_ref[...] = (_dot(a_ref[...], w_bf[...]) + b_ref[...]).astype(o_ref.dtype)

    @pl.when(m >= na_ref[0])
    def _():
        o_ref[...] = jnp.zeros_like(o_ref)


def _moe_experts(xs, block_e, n_active, w_gu, b_gu, w_dn, b_dn):
    n_rows = xs.shape[0]
    nb = n_rows // MOE_TM
    nc = D_EXPERT // MOE_TH
    b_gu3 = b_gu.reshape(N_EXPERTS, 1, 2 * D_EXPERT)
    b_dn3 = b_dn.reshape(N_EXPERTS, 1, D_MODEL)
    act = pl.pallas_call(
        _moe_up_kernel,
        out_shape=jax.ShapeDtypeStruct((n_rows, D_EXPERT), BF16),
        grid_spec=pltpu.PrefetchScalarGridSpec(
            num_scalar_prefetch=2,
            grid=(nc, nb),
            in_specs=[pl.BlockSpec((MOE_TM, D_MODEL), lambda c, m, be, na: (m, 0)),
                      pl.BlockSpec((None, D_MODEL, MOE_TH), lambda c, m, be, na: (be[m], 0, c)),
                      pl.BlockSpec((None, D_MODEL, MOE_TH), lambda c, m, be, na: (be[m], 0, nc + c)),
                      pl.BlockSpec((None, 1, MOE_TH), lambda c, m, be, na: (be[m], 0, c)),
                      pl.BlockSpec((None, 1, MOE_TH), lambda c, m, be, na: (be[m], 0, nc + c))],
            out_specs=pl.BlockSpec((MOE_TM, MOE_TH), lambda c, m, be, na: (m, c)),
            scratch_shapes=[pltpu.VMEM((D_MODEL, MOE_TH), BF16), pltpu.VMEM((D_MODEL, MOE_TH), BF16)],
        ),
        compiler_params=_params("arbitrary", "arbitrary"),
        name="moe_up",
    )(block_e, n_active, xs, w_gu, w_gu, b_gu3, b_gu3)
    nj = D_MODEL // MOE_TN
    return pl.pallas_call(
        _moe_down_kernel,
        out_shape=jax.ShapeDtypeStruct((n_rows, D_MODEL), BF16),
        grid_spec=pltpu.PrefetchScalarGridSpec(
            num_scalar_prefetch=2,
            grid=(nj, nb),
            in_specs=[pl.BlockSpec((MOE_TM, D_EXPERT), lambda j, m, be, na: (m, 0)),
                      pl.BlockSpec((None, D_EXPERT, MOE_TN), lambda j, m, be, na: (be[m], 0, j)),
                      pl.BlockSpec((None, 1, MOE_TN), lambda j, m, be, na: (be[m], 0, j))],
            out_specs=pl.BlockSpec((MOE_TM, MOE_TN), lambda j, m, be, na: (m, j)),
            scratch_shapes=[pltpu.VMEM((D_EXPERT, MOE_TN), BF16)],
        ),
        compiler_params=_params("arbitrary", "arbitrary"),
        name="moe_down",
    )(block_e, n_active, act, w_dn, b_dn3)


def _rank_kernel(e_ref, rank_ref, cnt_ref, run_ref, *, rb):
    @pl.when(pl.program_id(0) == 0)
    def _():
        run_ref[...] = jnp.zeros_like(run_ref)

    expert = lax.broadcasted_iota(jnp.int32, (N_EXPERTS, rb), 0)
    onehot = expert == e_ref[...]
    r = lax.broadcasted_iota(jnp.int32, (rb, rb), 0)
    c = lax.broadcasted_iota(jnp.int32, (rb, rb), 1)
    before = _dot(jnp.where(onehot, 1.0, 0.0).astype(BF16), (r < c).astype(BF16))
    run = run_ref[...]
    rank = jnp.sum(jnp.where(onehot, before + run, 0.0), axis=0, keepdims=True)
    rank_ref[...] = rank.astype(jnp.int32)
    run = run + jnp.sum(jnp.where(onehot, 1.0, 0.0), axis=1, keepdims=True)
    run_ref[...] = run
    cnt_ref[...] = jnp.broadcast_to(run, cnt_ref.shape)


def _expert_ranks(flat_e, rb=ROUTE_BLOCK):
    n = flat_e.shape[0]
    flat_e = jnp.pad(flat_e, (0, -n % rb), constant_values=N_EXPERTS)
    nb = flat_e.shape[0] // rb
    rank, cnt = pl.pallas_call(
        functools.partial(_rank_kernel, rb=rb),
        out_shape=(jax.ShapeDtypeStruct((nb, 1, rb), jnp.int32),
                   jax.ShapeDtypeStruct((N_EXPERTS, LANES), F32)),
        grid=(nb,),
        in_specs=[pl.BlockSpec((None, 1, rb), lambda i: (i, 0, 0))],
        out_specs=(pl.BlockSpec((None, 1, rb), lambda i: (i, 0, 0)),
                   pl.BlockSpec((N_EXPERTS, LANES), lambda i: (0, 0))),
        scratch_shapes=[pltpu.VMEM((N_EXPERTS, 1), F32)],
        compiler_params=_params("arbitrary"),
        name="expert_ranks",
    )(flat_e.reshape(nb, 1, rb))
    return rank.reshape(-1)[:n], cnt[:, 0].astype(jnp.int32)


def _route(logits, tm):
    n = logits.shape[0]
    n_assign = n * TOP_K
    n_blocks = -(-n_assign // tm) + N_EXPERTS
    top_logit, top_e = lax.top_k(logits, TOP_K)
    gates = jax.nn.softmax(top_logit, axis=-1).reshape(-1)
    flat_e = top_e.reshape(-1).astype(jnp.int32)
    rank, counts = _expert_ranks(flat_e)
    padded = (counts + tm - 1) // tm * tm
    pad_end = jnp.cumsum(padded)
    pad_start = pad_end - padded
    start = jnp.cumsum(counts) - counts
    onehot = flat_e[:, None] == jnp.arange(N_EXPERTS, dtype=jnp.int32)[None, :]
    slot_of = jnp.sum(jnp.where(onehot, pad_start[None, :], 0), axis=1) + rank
    order = jnp.argsort(flat_e, stable=True).astype(jnp.int32)
    blk0 = jnp.arange(n_blocks, dtype=jnp.int32) * tm
    block_e = jnp.minimum(jnp.sum((pad_end[None, :] <= blk0[:, None]).astype(jnp.int32), axis=1), N_EXPERTS - 1)
    base_rank = blk0 - pad_start[block_e]
    n_valid = jnp.clip(counts[block_e] - base_rank, 0, tm)
    src0 = start[block_e] + base_rank
    r = jnp.arange(tm, dtype=jnp.int32)[None, :]
    valid = r < n_valid[:, None]
    src = order[jnp.clip(src0[:, None] + r, 0, n_assign - 1)]
    slot_tok = jnp.where(valid, src // TOP_K, (blk0[:, None] + r) % n).reshape(-1)
    n_active = (pad_end[-1] // tm).astype(jnp.int32).reshape(1)
    return slot_tok, block_e, n_active, slot_of.reshape(n, TOP_K), gates.reshape(n, TOP_K)


def _combine_ln_kernel(x1_ref, y_ref, gt_ref, ga_ref, g_ref, b_ref, o_ref):
    gates = gt_ref[...]
    ffn = y_ref[0].astype(F32) * gates[:, 0:1]
    for k in range(1, TOP_K):
        ffn = ffn + y_ref[k].astype(F32) * gates[:, k:k + 1]
    o_ref[...] = _layer_norm(DEEPNORM_ALPHA * x1_ref[...] + ga_ref[...] * ffn, g_ref[...], b_ref[...])


def _combine_ln(x1, y4, gates, row0, mod, ln_g, ln_b, *, tm, name):
    n = x1.shape[0]
    rb = row0 // tm
    row = lambda i: i
    blk = pl.BlockSpec((tm, D_MODEL), lambda i: (i, 0))
    vec = pl.BlockSpec((1, D_MODEL), lambda i: (0, 0))
    return pl.pallas_call(
        _combine_ln_kernel,
        out_shape=jax.ShapeDtypeStruct((n, D_MODEL), F32),
        grid=(n // tm,),
        in_specs=[blk, pl.BlockSpec((TOP_K, tm, D_MODEL), lambda i: (0, i + rb, 0)),
                  pl.BlockSpec((tm, TOP_K), lambda i: (i + rb, 0)),
                  mod.spec(5, tm, row), vec, vec],
        out_specs=blk,
        compiler_params=_params("parallel"),
        name=name,
    )(x1, y4, gates, mod.arr, ln_g, ln_b)


def _dense_front(x, mod, w, ln_in_g, ln_in_b, *, tm, tag):
    xn, h = _ln_mod(x, ln_in_g, ln_in_b, mod, tm=tm, name=f"ln_mod_{tag}")
    mm = functools.partial(_matmul, h, w["w_in_main"], tm=tm)
    q = mm(tn=1024, out_dtype=F32, n_out=BRANCH_WIDTH, col0=0, name=f"proj_q_{tag}")
    k = mm(tn=1024, out_dtype=F32, n_out=BRANCH_WIDTH, col0=BRANCH_WIDTH, name=f"proj_k_{tag}")
    v = mm(tn=1024, out_dtype=F32, n_out=BRANCH_WIDTH, col0=2 * BRANCH_WIDTH, name=f"proj_v_{tag}")
    dnz = mm(tn=1024, out_dtype=F32, n_out=4 * BRANCH_WIDTH, col0=3 * BRANCH_WIDTH, name=f"proj_dn_{tag}")
    mt = functools.partial(_matmul, h, w["w_in_tail"], tm=tm)
    gates = mt(tn=1024, out_dtype=F32, n_out=2 * D_MODEL, col0=0, name=f"proj_gate_{tag}")
    ba = mt(tn=LANES, out_dtype=F32, n_out=LANES, col0=2 * D_MODEL, name=f"proj_ba_{tag}")
    return xn, q, k, v, dnz, gates, ba


def _dense_back(o_sb, o_dn, gates, xn, mod, w, *, tm, tm_out, tag):
    merged = _merge(o_sb, o_dn, w["w_branch_sb"], w["w_branch_dn"], gates, tm=tm, tn=1024, name=f"merge_{tag}")
    return _out_ln(merged, w["w_out"], xn, mod, w["ln1_g"], w["ln1_b"], w["w_router"], w["b_router"],
                   tm=tm_out, name=f"out_ln_{tag}")


def _logit_rows(ba, n_seq, seq, chunk):
    t = ba[:, :2 * N_HEADS].reshape(n_seq * seq // chunk, chunk, 2 * N_HEADS)
    return t.transpose(0, 2, 1)


def kernel(x_prompt, x_sample, cache_k, cache_v, state_dn, state_conv, page_table, c_prompt, c_sample,
           ln_in_g, ln_in_b, w_ada, b_ada, w_in, sb_bias, conv_w, dn_a_log, dn_dt_bias, dn_norm_w,
           w_branch_sb, w_branch_dn, w_out, ln1_g, ln1_b, w_router, b_router, w_gu, b_gu,
           w_dn, b_dn, ln2_g, ln2_b):
    assert w_ada.shape[0] == DEPTH == 1
    bp, sp, d = x_prompt.shape
    bs, ss, _ = x_sample.shape
    n_p, n_s = bp * sp, bs * ss
    l = 0

    w_in_l = w_in[l]
    w_in_tail = jnp.concatenate(
        [w_in_l[:, OFF_GATE:], w_in_l[:, OFF_DN_B:OFF_GATE],
         jnp.zeros((d, LANES - 2 * N_HEADS), w_in.dtype)], axis=1).astype(BF16)
    w = {
        "w_in_main": w_in_l, "w_in_tail": w_in_tail,
        "w_branch_sb": w_branch_sb[l].astype(BF16), "w_branch_dn": w_branch_dn[l].astype(BF16),
        "w_out": w_out[l].astype(BF16),
        "ln1_g": ln1_g[l].reshape(1, d), "ln1_b": ln1_b[l].reshape(1, d),
        "w_router": jnp.pad(w_router[l], ((0, 0), (0, LANES - N_EXPERTS))),
        "b_router": jnp.pad(b_router[l], (0, LANES - N_EXPERTS)).reshape(1, LANES),
    }
    ln2g, ln2b = ln2_g[l].reshape(1, d), ln2_b[l].reshape(1, d)
    conv_w8 = jnp.pad(conv_w[l], ((0, SUBLANES - DN_CONV), (0, 0)))
    norm_w = dn_norm_w[l].reshape(1, HEAD_DIM)
    a_scale = -jnp.exp(dn_a_log[l].astype(F32))

    c_all = jnp.concatenate([c_prompt, c_sample], axis=0)
    mod_all = _matmul(c_all, w_ada[l], tm=c_all.shape[0], tn=1024, out_dtype=F32, n_out=6 * d,
                      bias=b_ada[l].reshape(1, 6 * d), silu_in=True, name="ada")
    mod_p = _Mod(mod_all[:bp], sp, n_p)
    mod_s = _Mod(mod_all[bp:], ss, n_s)

    xn_p, q_p, k_p, v_p, dnz_p, gates_p, ba_p = _dense_front(
        x_prompt.reshape(n_p, d), mod_p, w, ln_in_g, ln_in_b, tm=min(1024, n_p), tag="p")
    o_sb_p = _sb_prompt(q_p, k_p, v_p, sb_bias[l], n_seq=bp, seq=sp)
    zero_prev = jnp.zeros((bp, SUBLANES, 3 * BRANCH_WIDTH), F32)
    dq, dk, dv = _dn_prep(dnz_p, zero_prev, conv_w8, n_seq=bp, seq=sp, valid=sp, name="dn_prep_p")
    o_dn_p, s_p = _dn_chunks(dq, dk, dv, dnz_p, 3 * BRANCH_WIDTH, ba_p, _logit_rows(ba_p, bp, sp, DN_CHUNK),
                             a_scale, dn_dt_bias[l], norm_w,
                             jnp.zeros((bp, N_HEADS, HEAD_DIM, HEAD_DIM), F32),
                             n_seq=bp, seq=sp, chunk=DN_CHUNK, rows=4 * DN_CHUNK)
    x1_p, h2_p, lg_p = _dense_back(o_sb_p, o_dn_p, gates_p, xn_p, mod_p, w, tm=min(1024, n_p), tm_out=256,
                                   tag="p")

    xn_s, q_s, k_s, v_s, dnz_s, gates_s, ba_s = _dense_front(
        x_sample.reshape(n_s, d), mod_s, w, ln_in_g, ln_in_b, tm=n_s, tag="s")
    pad_q = SUBLANES - ss

    def heads_rows(t):
        t = t.reshape(bs, ss, N_HEADS, HEAD_DIM).transpose(0, 2, 1, 3)
        return (jnp.pad(t, ((0, 0), (0, 0), (0, SAMPLE_QROWS - ss), (0, 0)))
                .reshape(bs, N_HEADS * SAMPLE_QROWS, HEAD_DIM))

    def new_page(t):
        t = jnp.pad(t.reshape(bs, ss, BRANCH_WIDTH), ((0, 0), (0, PAGE_SIZE - ss), (0, 0)))
        return t.reshape(bs, PAGE_SIZE * N_HEADS, HEAD_DIM)

    n_phys = cache_k.shape[1]
    o_s = _sb_sample(heads_rows(q_s), new_page(k_s), new_page(v_s), sb_bias[l],
                     cache_k.reshape(-1, PAGE_SIZE * N_HEADS, HEAD_DIM),
                     cache_v.reshape(-1, PAGE_SIZE * N_HEADS, HEAD_DIM), page_table, l * n_phys)
    o_sb_s = (o_s.reshape(bs, N_HEADS, SAMPLE_QROWS, HEAD_DIM)[:, :, :ss].transpose(0, 2, 1, 3)
              .reshape(n_s, BRANCH_WIDTH).astype(BF16))

    def pad_rows(t):
        return jnp.pad(t.reshape(bs, ss, -1), ((0, 0), (0, pad_q), (0, 0))).reshape(bs * SUBLANES, -1)

    dnz_s8 = pad_rows(dnz_s)
    ba_s8 = pad_rows(ba_s)
    prev_s = jnp.pad(state_conv[l], ((0, 0), (SUBLANES - (DN_CONV - 1), 0), (0, 0)))
    dq, dk, dv = _dn_prep(dnz_s8, prev_s, conv_w8, n_seq=bs, seq=SUBLANES, valid=ss, name="dn_prep_s")
    tok = jnp.arange(bs * SUBLANES) % SUBLANES
    neutral = jnp.concatenate([jnp.full((N_HEADS,), -1e30, F32), jnp.full((N_HEADS,), -1e30, F32),
                               jnp.zeros((LANES - 2 * N_HEADS,), F32)])
    ba_s8 = jnp.where((tok < ss)[:, None], ba_s8, neutral[None, :])
    o_dn_s8, s_s = _dn_chunks(dq, dk, dv, dnz_s8, 3 * BRANCH_WIDTH, ba_s8, _logit_rows(ba_s8, bs, SUBLANES, SUBLANES),
                              a_scale, dn_dt_bias[l], norm_w, state_dn[l],
                              n_seq=bs, seq=SUBLANES, chunk=SUBLANES, rows=SUBLANES)
    o_dn_s = o_dn_s8.reshape(bs, SUBLANES, BRANCH_WIDTH)[:, :ss].reshape(n_s, BRANCH_WIDTH)
    x1_s, h2_s, lg_s = _dense_back(o_sb_s, o_dn_s, gates_s, xn_s, mod_s, w, tm=n_s, tm_out=n_s, tag="s")

    logits = jnp.concatenate([lg_p[:, :N_EXPERTS], lg_s[:, :N_EXPERTS]], axis=0)
    h2 = jnp.concatenate([h2_p, h2_s], axis=0)
    slot_tok, block_e, n_active, slot_of, gates = _route(logits, MOE_TM)
    xs = h2.at[slot_tok].get(mode="promise_in_bounds")
    yb = _moe_experts(xs, block_e, n_active, w_gu[l], b_gu[l], w_dn[l], b_dn[l])
    y4 = yb.at[slot_of.T].get(mode="promise_in_bounds")
    y_p = _combine_ln(x1_p, y4, gates, 0, mod_p, ln2g, ln2b, tm=256, name="combine_ln_p")
    y_s = _combine_ln(x1_s, y4, gates, n_p, mod_s, ln2g, ln2b, tm=n_s, name="combine_ln_s")

    heads = lambda t, b, s: t.reshape(1, b, s, N_HEADS, HEAD_DIM)
    conv_p = dnz_p.reshape(bp, sp, 4 * BRANCH_WIDTH)[:, sp - (DN_CONV - 1):, :3 * BRANCH_WIDTH]
    xp_s = jnp.concatenate([state_conv[l], dnz_s.reshape(bs, ss, 4 * BRANCH_WIDTH)[:, :, :3 * BRANCH_WIDTH]], axis=1)
    conv_s = xp_s[:, -(DN_CONV - 1):]
    return (y_p.reshape(bp, sp, d), y_s.reshape(bs, ss, d),
            heads(k_p, bp, sp), heads(v_p, bp, sp), s_p[None], conv_p[None],
            heads(k_s, bs, ss), heads(v_s, bs, ss), s_s[None], conv_s[None])
```

```python
import functools
import math

import jax
import jax.numpy as jnp
from jax import lax
from jax.experimental import pallas as pl
from jax.experimental.pallas import tpu as pltpu

F32 = jnp.float32
BF16 = jnp.bfloat16

D_MODEL = 2048
HEAD_DIM = 128
N_HEADS = 8
BRANCH_WIDTH = N_HEADS * HEAD_DIM
PAGE_SIZE = 128
DN_CONV = 4
DN_CHUNK = 64
N_EXPERTS = 32
TOP_K = 4
D_EXPERT = D_MODEL
SWIGLU_LIMIT = 7.0
SWIGLU_ALPHA = 1.702
LN_EPS = 1e-5
RMS_EPS = 1e-6
L2_EPS = 1e-6
DEPTH = 1
DEEPNORM_ALPHA = (2.0 * DEPTH) ** 0.25
OFF_DN_QKV = 3 * BRANCH_WIDTH
OFF_DN_Z = OFF_DN_QKV + 3 * BRANCH_WIDTH
OFF_DN_B = OFF_DN_Z + BRANCH_WIDTH
OFF_GATE = OFF_DN_B + 2 * N_HEADS
N_IN = OFF_GATE + 2 * D_MODEL

LANES = 128
SUBLANES = 8
VMEM_LIMIT = 48 * 1024 * 1024
MOE_TM = 512
MOE_TH = 512
MOE_TN = 1024
ROUTE_BLOCK = 512
SB_PROMPT_TILE = 256
SB_PROMPT_HEADS = 4
SAMPLE_PAGES_PER_STEP = 16
SAMPLE_QROWS = 16


def _params(*sem):
    return pltpu.CompilerParams(dimension_semantics=sem, vmem_limit_bytes=VMEM_LIMIT)


def _dot(a, b):
    return jnp.dot(a, b, preferred_element_type=F32)


def _dot_nt(a, b):
    return lax.dot_general(a, b, (((1,), (1,)), ((), ())), preferred_element_type=F32)


def _dot_tn(a, b):
    return lax.dot_general(a, b, (((0,), (0,)), ((), ())), preferred_element_type=F32)


def _split(x):
    hi = x.astype(BF16)
    lo = (x - hi.astype(F32)).astype(BF16)
    return hi, lo


def _dot3(a, b):
    ah, al = _split(a)
    bh, bl = _split(b)
    return _dot(ah, bh) + (_dot(ah, bl) + _dot(al, bh))


def _sigmoid(x):
    return 1.0 / (1.0 + jnp.exp(-x))


def _softplus(x):
    return jnp.maximum(x, 0.0) + jnp.log1p(jnp.exp(-jnp.abs(x)))


def _layer_norm(x, g, b):
    mu = jnp.mean(x, axis=-1, keepdims=True)
    xc = x - mu
    var = jnp.mean(xc * xc, axis=-1, keepdims=True)
    return xc * lax.rsqrt(var + LN_EPS) * g + b


class _Mod:
    def __init__(self, mod, rows_per_seq, n_rows):
        n_seq = mod.shape[0]
        self.grouped = rows_per_seq % SUBLANES == 0 and rows_per_seq >= LANES
        self.rows_per_seq = rows_per_seq
        m6 = mod.reshape(n_seq, 6, D_MODEL).transpose(1, 0, 2)
        if self.grouped:
            self.arr = m6.reshape(6, n_seq, 1, D_MODEL)
        else:
            self.arr = jnp.repeat(m6, rows_per_seq, axis=1)
            assert self.arr.shape[1] == n_rows

    def spec(self, comp, tm, row_block_of):
        if self.grouped:
            bps = self.rows_per_seq // tm
            return pl.BlockSpec((None, None, 1, D_MODEL),
                                lambda *g: (comp, row_block_of(*g) // bps, 0, 0))
        return pl.BlockSpec((None, tm, D_MODEL), lambda *g: (comp, row_block_of(*g), 0))


def _mm_kernel(x_ref, w_ref, *rest, has_bias, silu_in):
    if has_bias:
        b_ref, o_ref = rest
    else:
        (o_ref,) = rest
    x = x_ref[...]
    if silu_in:
        x = x * _sigmoid(x)
    acc = _dot(x.astype(BF16), w_ref[...].astype(BF16))
    if has_bias:
        acc = acc + b_ref[...]
    o_ref[...] = acc.astype(o_ref.dtype)


def _matmul(x, w, *, tm, tn, out_dtype, n_out, col0=0, bias=None, silu_in=False, name):
    n, k = x.shape
    assert n % tm == 0 and n_out % tn == 0 and col0 % tn == 0
    cb = col0 // tn
    in_specs = [pl.BlockSpec((tm, k), lambda i, j: (i, 0)),
                pl.BlockSpec((k, tn), lambda i, j: (0, j + cb))]
    args = [x, w]
    if bias is not None:
        in_specs.append(pl.BlockSpec((1, tn), lambda i, j: (0, j + cb)))
        args.append(bias)
    return pl.pallas_call(
        functools.partial(_mm_kernel, has_bias=bias is not None, silu_in=silu_in),
        out_shape=jax.ShapeDtypeStruct((n, n_out), out_dtype),
        grid=(n // tm, n_out // tn),
        in_specs=in_specs,
        out_specs=pl.BlockSpec((tm, tn), lambda i, j: (i, j)),
        compiler_params=_params("parallel", "arbitrary"),
        name=name,
    )(*args)


def _ln_mod_kernel(x_ref, g_ref, b_ref, sc_ref, sh_ref, xn_ref, h_ref):
    xn = _layer_norm(x_ref[...], g_ref[...], b_ref[...])
    xn_ref[...] = xn
    h_ref[...] = (xn * (1.0 + sc_ref[...]) + sh_ref[...]).astype(BF16)


def _ln_mod(x, g, b, mod, *, tm, name):
    n = x.shape[0]
    row = lambda i: i
    vec = pl.BlockSpec((1, D_MODEL), lambda i: (0, 0))
    blk = pl.BlockSpec((tm, D_MODEL), lambda i: (i, 0))
    return pl.pallas_call(
        _ln_mod_kernel,
        out_shape=(jax.ShapeDtypeStruct((n, D_MODEL), F32), jax.ShapeDtypeStruct((n, D_MODEL), BF16)),
        grid=(n // tm,),
        in_specs=[blk, vec, vec, mod.spec(1, tm, row), mod.spec(0, tm, row)],
        out_specs=(blk, blk),
        compiler_params=_params("parallel"),
        name=name,
    )(x, g.reshape(1, D_MODEL), b.reshape(1, D_MODEL), mod.arr, mod.arr)


def _stick_blocks(qs, kbs, vbs, biases, log_afters, later_mat, mask):
    n = range(len(qs))
    zs = [_dot_nt(qs[i], kbs[i]) + biases[i] for i in n]
    log_keep, log_beta = [], []
    for z in zs:
        t = jnp.log(1.0 + jnp.exp(-jnp.abs(z)))
        lk = -(jnp.maximum(z, 0.0) + t)
        log_keep.append(lk if mask is None else jnp.where(mask, lk, 0.0))
        log_beta.append(jnp.minimum(z, 0.0) - t)
    parts = [_split(lk) for lk in log_keep]
    within = [_dot(hi, later_mat) + _dot(lo, later_mat) for hi, lo in parts]
    ws = []
    for i in n:
        w = jnp.exp(log_beta[i] + (within[i] + log_afters[i]))
        ws.append((w if mask is None else jnp.where(mask, w, 0.0)).astype(BF16))
    outs = [_dot(ws[i], vbs[i]) for i in n]
    new_after = [log_afters[i] + jnp.sum(log_keep[i], axis=-1, keepdims=True) for i in n]
    return outs, new_after


def _later_matrix(tk):
    r = lax.broadcasted_iota(jnp.int32, (tk, tk), 0)
    c = lax.broadcasted_iota(jnp.int32, (tk, tk), 1)
    return (r > c).astype(BF16)


def _sb_prompt_kernel(bias_ref, q_ref, k_ref, v_ref, o_ref, *, t, hp, scale):
    hg = pl.program_id(1)
    qi = pl.program_id(2)
    heads = range(hp)
    hcols = [slice(i * HEAD_DIM, (i + 1) * HEAD_DIM) for i in heads]
    biases = [bias_ref[hg * hp + i] for i in heads]
    qs = [(q_ref[:, hc] * scale).astype(BF16) for hc in hcols]
    later_mat = _later_matrix(t)
    r = lax.broadcasted_iota(jnp.int32, (t, t), 0)
    c = lax.broadcasted_iota(jnp.int32, (t, t), 1)

    def kv(j):
        rows = pl.ds(pl.multiple_of(j * t, t), t)
        return ([k_ref[rows, hc].astype(BF16) for hc in hcols],
                [v_ref[rows, hc].astype(BF16) for hc in hcols])

    kbs, vbs = kv(qi)
    zero = [jnp.zeros((t, 1), F32) for _ in heads]
    accs, afters = _stick_blocks(qs, kbs, vbs, biases, zero, later_mat, c < r)

    def body(step, carry):
        accs, afters = carry
        kbs, vbs = kv(qi - 1 - step)
        outs, afters = _stick_blocks(qs, kbs, vbs, biases, list(afters), later_mat, None)
        return tuple(a + o for a, o in zip(accs, outs)), tuple(afters)

    accs, _ = lax.fori_loop(0, qi, body, (tuple(accs), tuple(afters)))
    for i in heads:
        o_ref[:, hcols[i]] = accs[i].astype(o_ref.dtype)


def _sb_prompt(q, k, v, bias, *, n_seq, seq, t=SB_PROMPT_TILE, hp=SB_PROMPT_HEADS):
    nq = seq // t
    w = hp * HEAD_DIM
    return pl.pallas_call(
        functools.partial(_sb_prompt_kernel, t=t, hp=hp, scale=HEAD_DIM ** -0.5),
        out_shape=jax.ShapeDtypeStruct(q.shape, BF16),
        grid=(n_seq, N_HEADS // hp, nq),
        in_specs=[pl.BlockSpec(memory_space=pltpu.SMEM),
                  pl.BlockSpec((t, w), lambda b, h, i: (b * nq + i, h)),
                  pl.BlockSpec((seq, w), lambda b, h, i: (b, h)),
                  pl.BlockSpec((seq, w), lambda b, h, i: (b, h))],
        out_specs=pl.BlockSpec((t, w), lambda b, h, i: (b * nq + i, h)),
        compiler_params=_params("parallel", "parallel", "arbitrary"),
        name="sb_prompt",
    )(bias, q, k, v)


def _sb_sample_kernel(pt_ref, bias_ref, q_ref, kn_ref, vn_ref, *rest, n_pg, scale):
    k_refs = rest[:n_pg]
    v_refs = rest[n_pg:2 * n_pg]
    o_ref = rest[2 * n_pg]
    acc_ref, la_ref = rest[2 * n_pg + 1:]
    step = pl.program_id(1)
    qr = SAMPLE_QROWS
    rows = N_HEADS * qr
    later_mat = _later_matrix(PAGE_SIZE)
    bias = jnp.concatenate(
        [jnp.full((qr, 1), bias_ref[h], F32) for h in range(N_HEADS)], axis=0)
    q = (q_ref[...] * scale).astype(BF16)

    def head_rows(page_ref, h):
        return page_ref[pl.ds(h, PAGE_SIZE, stride=N_HEADS), :].astype(BF16)

    def pages(k_list, v_list, mask):
        n = len(k_list)
        hrows = [slice(h * qr, (h + 1) * qr) for h in range(N_HEADS)]
        page_lanes = [slice(i * PAGE_SIZE, (i + 1) * PAGE_SIZE) for i in range(n)]
        zs = []
        for h in range(N_HEADS):
            k_cat = jnp.concatenate([head_rows(k_ref, h) for k_ref in k_list], axis=0)
            zs.append(_dot_nt(q[hrows[h]], k_cat))
        z = jnp.concatenate(zs, axis=0) + bias
        t = jnp.log(1.0 + jnp.exp(-jnp.abs(z)))
        log_keep = -(jnp.maximum(z, 0.0) + t)
        log_beta = jnp.minimum(z, 0.0) - t
        if mask is not None:
            log_keep = jnp.where(mask, log_keep, 0.0)
        hi, lo = _split(jnp.concatenate([log_keep[:, pl_] for pl_ in page_lanes], axis=0))
        within = _dot(hi, later_mat) + _dot(lo, later_mat)
        la = la_ref[...]
        later = []
        for i in range(n):
            later.append(within[i * rows:(i + 1) * rows] + la)
            la = la + jnp.sum(log_keep[:, page_lanes[i]], axis=-1, keepdims=True)
        la_ref[...] = la
        w = jnp.exp(log_beta + jnp.concatenate(later, axis=1))
        if mask is not None:
            w = jnp.where(mask, w, 0.0)
        w = w.astype(BF16)
        for h in range(N_HEADS):
            v_cat = jnp.concatenate([head_rows(v_ref, h) for v_ref in v_list], axis=0)
            acc_ref[hrows[h], :] += _dot(w[hrows[h]], v_cat)

    @pl.when(step == 0)
    def _():
        acc_ref[...] = jnp.zeros_like(acc_ref)
        la_ref[...] = jnp.zeros_like(la_ref)
        r = lax.broadcasted_iota(jnp.int32, (rows, PAGE_SIZE), 0) % qr
        c = lax.broadcasted_iota(jnp.int32, (rows, PAGE_SIZE), 1)
        pages([kn_ref], [vn_ref], c < r)

    @pl.when(step > 0)
    def _():
        pages(list(k_refs), list(v_refs), None)

    @pl.when(step == pl.num_programs(1) - 1)
    def _():
        o_ref[...] = acc_ref[...]


def _sb_sample(q, k_new, v_new, bias, cache_k, cache_v, page_table, page0):
    bd, n_pages = page_table.shape
    n_pg = SAMPLE_PAGES_PER_STEP
    assert n_pages % n_pg == 0
    n_steps = n_pages // n_pg
    page_rows = PAGE_SIZE * N_HEADS
    rows = N_HEADS * SAMPLE_QROWS

    def page_spec(i):
        def index(b, s, pt):
            logical = n_pages - 1 - (jnp.maximum(s - 1, 0) * n_pg + i)
            return (pt[b, logical] + page0, 0, 0)
        return pl.BlockSpec((None, page_rows, HEAD_DIM), index)

    new_spec = pl.BlockSpec((None, page_rows, HEAD_DIM), lambda b, s, pt: (b, 0, 0))
    q_spec = pl.BlockSpec((None, rows, HEAD_DIM), lambda b, s, pt: (b, 0, 0))
    grid_spec = pltpu.PrefetchScalarGridSpec(
        num_scalar_prefetch=1,
        grid=(bd, n_steps + 1),
        in_specs=[pl.BlockSpec(memory_space=pltpu.SMEM), q_spec, new_spec, new_spec]
        + [page_spec(i) for i in range(n_pg)] * 2,
        out_specs=q_spec,
        scratch_shapes=[pltpu.VMEM((rows, HEAD_DIM), F32), pltpu.VMEM((rows, 1), F32)],
    )
    return pl.pallas_call(
        functools.partial(_sb_sample_kernel, n_pg=n_pg, scale=HEAD_DIM ** -0.5),
        out_shape=jax.ShapeDtypeStruct((bd, rows, HEAD_DIM), F32),
        grid_spec=grid_spec,
        compiler_params=_params("parallel", "arbitrary"),
        name="sb_sample",
    )(page_table, bias, q, k_new, v_new, *([cache_k] * n_pg), *([cache_v] * n_pg))


def _dn_prep_kernel(xq_ref, xk_ref, xv_ref, pq_ref, pk_ref, pv_ref, wq_ref, wk_ref, wv_ref,
                    oq_ref, ok_ref, ov_ref, *, seq, valid):
    width = xq_ref.shape[1]
    rows8 = lax.broadcasted_iota(jnp.int32, (SUBLANES, width), 0)

    def conv_silu(x_ref, p_ref, w_ref):
        w = w_ref[...]
        x8 = x_ref[0:SUBLANES, :]
        p8 = p_ref[...]
        head = x8 * w[DN_CONV - 1:DN_CONV]
        for k in range(1, DN_CONV):
            shifted = jnp.where(rows8 < k, pltpu.roll(p8, k, 0), pltpu.roll(x8, k, 0))
            head = head + shifted * w[DN_CONV - 1 - k:DN_CONV - k]
        if seq == SUBLANES:
            acc = head
        else:
            x = x_ref[...]
            acc = x * w[DN_CONV - 1:DN_CONV]
            for k in range(1, DN_CONV):
                acc = acc + pltpu.roll(x, k, 0) * w[DN_CONV - 1 - k:DN_CONV - k]
            acc = jnp.concatenate([head, acc[SUBLANES:]], axis=0)
        y = acc * _sigmoid(acc)
        if valid < seq:
            r = lax.broadcasted_iota(jnp.int32, y.shape, 0)
            y = jnp.where(r < valid, y, 0.0)
        return y

    def l2n(y):
        parts = [y[:, i:i + HEAD_DIM] for i in range(0, width, HEAD_DIM)]
        parts = [p * lax.rsqrt(jnp.sum(p * p, axis=-1, keepdims=True) + L2_EPS) for p in parts]
        return parts[0] if len(parts) == 1 else jnp.concatenate(parts, axis=1)

    oq_ref[...] = l2n(conv_silu(xq_ref, pq_ref, wq_ref)) * (HEAD_DIM ** -0.5)
    ok_ref[...] = l2n(conv_silu(xk_ref, pk_ref, wk_ref))
    ov_ref[...] = conv_silu(xv_ref, pv_ref, wv_ref)


def _dn_prep(x, prev, conv_w, *, n_seq, seq, valid, heads_per_step, name):
    nh = N_HEADS // heads_per_step
    wd = heads_per_step * HEAD_DIM
    xs = [pl.BlockSpec((seq, wd), lambda b, h, s=s: (b, s * nh + h)) for s in range(3)]
    ps = [pl.BlockSpec((None, SUBLANES, wd), lambda b, h, s=s: (b, 0, s * nh + h)) for s in range(3)]
    ws = [pl.BlockSpec((SUBLANES, wd), lambda b, h, s=s: (0, s * nh + h)) for s in range(3)]
    out = pl.BlockSpec((seq, wd), lambda b, h: (b, h))
    shape = jax.ShapeDtypeStruct((n_seq * seq, BRANCH_WIDTH), F32)
    return pl.pallas_call(
        functools.partial(_dn_prep_kernel, seq=seq, valid=valid),
        out_shape=(shape, shape, shape),
        grid=(n_seq, nh),
        in_specs=xs + ps + ws,
        out_specs=(out, out, out),
        compiler_params=_params("parallel", "parallel"),
        name=name,
    )(x, x, x, prev, prev, prev, conv_w, conv_w, conv_w)


def _mm3(x, y):
    (xh, xl), (yh, yl) = x, y
    return _dot(xh, yh) + (_dot(xh, yl) + _dot(xl, yh))


def _unit_lower_inverse_each(mats, c):
    r = lax.broadcasted_iota(jnp.int32, (c, c), 0)
    col = lax.broadcasted_iota(jnp.int32, (c, c), 1)
    eye = jnp.where(r == col, 1.0, 0.0)
    p = [eye - a for a in mats]
    a_s = [_split(a) for a in mats]
    power = [_mm3(s, s) for s in a_s]
    for _ in range(int(math.log2(c)) - 2):
        pw_s = [_split(x) for x in power]
        p_s = [_split(x) for x in p]
        power = [_mm3(s, s) for s in pw_s]
        upd = [_mm3(ps, ws) for ps, ws in zip(p_s, pw_s)]
        p = [x + y for x, y in zip(p, upd)]
    pw_s = [_split(x) for x in power]
    p_s = [_split(x) for x in p]
    return [x + _mm3(ps, ws) for x, ps, ws in zip(p, p_s, pw_s)]


def _dn_chunk_kernel(ascale_ref, dtb_ref, q_ref, k_ref, v_ref, z_ref, ba_ref, bat_ref, nw_ref, s0_ref,
                     o_ref, s_ref, *, chunk, n_chunks):
    c = chunk
    heads = range(N_HEADS)

    @pl.when(pl.program_id(1) == 0)
    def _():
        s_ref[...] = s0_ref[...]

    r = lax.broadcasted_iota(jnp.int32, (c, c), 0)
    col = lax.broadcasted_iota(jnp.int32, (c, c), 1)
    tri = r >= col
    strict = r > col
    nw = nw_ref[...]

    def one_chunk(ci, carry):
        rows = pl.ds(pl.multiple_of(ci * c, c), c)
        bat = bat_ref[ci]
        hcols = [slice(h * HEAD_DIM, (h + 1) * HEAD_DIM) for h in heads]
        q = [q_ref[rows, hc] for hc in hcols]
        k = [k_ref[rows, hc] for hc in hcols]
        v = [v_ref[rows, hc] for hc in hcols]
        kb = [x.astype(BF16) for x in k]
        kk = [_dot_nt(x, x) for x in kb]
        qk = [_dot_nt(x.astype(BF16), y) for x, y in zip(q, kb)]
        beta, gc_col, g_last, decay, e_col = [], [], [], [], []
        for h in heads:
            a_scale = ascale_ref[h]
            beta.append(_sigmoid(ba_ref[rows, h:h + 1]))
            g_col = a_scale * _softplus(ba_ref[rows, N_HEADS + h:N_HEADS + h + 1] + dtb_ref[h])
            g_row = a_scale * _softplus(bat[N_HEADS + h:N_HEADS + h + 1, :] + dtb_ref[h])
            gcc = jnp.sum(jnp.where(tri, jnp.broadcast_to(g_row, (c, c)), 0.0), axis=1, keepdims=True)
            gcr = jnp.sum(jnp.where(r <= col, jnp.broadcast_to(g_col, (c, c)), 0.0), axis=0, keepdims=True)
            gc_col.append(gcc)
            g_last.append(jnp.sum(g_row, axis=1, keepdims=True))
            decay.append(jnp.exp(jnp.where(tri, gcc - gcr, -1e30)))
            e_col.append(jnp.exp(gcc))
        a_low = [jnp.where(strict, kk[h] * decay[h], 0.0) * beta[h] for h in heads]
        t_inv = _unit_lower_inverse_each(a_low, c)
        rhs = [jnp.concatenate([v[h] * beta[h], k[h] * (beta[h] * e_col[h])], axis=1) for h in heads]
        t_s = [_split(x) for x in t_inv]
        rhs_s = [_split(x) for x in rhs]
        sol = [_mm3(t_s[h], rhs_s[h]) for h in heads]
        state = [s_ref[h] for h in heads]
        sb = [x.astype(BF16) for x in state]
        u = [sol[h][:, :HEAD_DIM] - _dot(sol[h][:, HEAD_DIM:].astype(BF16), sb[h]) for h in heads]
        o_state = [_dot((q[h] * e_col[h]).astype(BF16), sb[h]) for h in heads]
        ub = [x.astype(BF16) for x in u]
        o = [o_state[h] + _dot(jnp.where(tri, qk[h] * decay[h], 0.0).astype(BF16), ub[h]) for h in heads]
        k_end = [(k[h] * jnp.exp(g_last[h] - gc_col[h])).astype(BF16) for h in heads]
        s_new = [state[h] * jnp.exp(g_last[h]) + _dot_tn(k_end[h], ub[h]) for h in heads]
        for h in heads:
            s_ref[h] = s_new[h]
            on = o[h] * lax.rsqrt(jnp.mean(o[h] * o[h], axis=-1, keepdims=True) + RMS_EPS)
            zz = z_ref[rows, hcols[h]]
            o_ref[rows, hcols[h]] = (on * nw * (zz * _sigmoid(zz))).astype(o_ref.dtype)
        return carry

    lax.fori_loop(0, n_chunks, one_chunk, 0)


def _dn_chunks(q, k, v, z_arr, z_col0, ba, bat, a_scale, dt_bias, norm_w, s0, *, n_seq, seq, chunk, rows):
    nr = seq // rows
    ncb = rows // chunk
    w = BRANCH_WIDTH
    zb = z_col0 // w
    blk = pl.BlockSpec((rows, w), lambda b, r: (b * nr + r, 0))
    state = pl.BlockSpec((None, N_HEADS, HEAD_DIM, HEAD_DIM), lambda b, r: (b, 0, 0, 0))
    smem = pl.BlockSpec(memory_space=pltpu.SMEM)
    return pl.pallas_call(
        functools.partial(_dn_chunk_kernel, chunk=chunk, n_chunks=ncb),
        out_shape=(jax.ShapeDtypeStruct((n_seq * seq, w), BF16),
                   jax.ShapeDtypeStruct((n_seq, N_HEADS, HEAD_DIM, HEAD_DIM), F32)),
        grid=(n_seq, nr),
        in_specs=[smem, smem, blk, blk, blk,
                  pl.BlockSpec((rows, w), lambda b, r: (b * nr + r, zb)),
                  pl.BlockSpec((rows, LANES), lambda b, r: (b * nr + r, 0)),
                  pl.BlockSpec((ncb, 2 * N_HEADS, chunk), lambda b, r: (b * nr + r, 0, 0)),
                  pl.BlockSpec((1, HEAD_DIM), lambda b, r: (0, 0)),
                  state],
        out_specs=(blk, state),
        compiler_params=_params("parallel", "arbitrary"),
        name=f"dn_chunks_{chunk}",
    )(a_scale, dt_bias, q, k, v, z_arr, ba, bat, norm_w, s0)


def _merge_kernel(osb_ref, odn_ref, wsb_ref, wdn_ref, gsb_ref, gdn_ref, o_ref):
    a = _dot(osb_ref[...], wsb_ref[...])
    b = _dot(odn_ref[...], wdn_ref[...])
    o_ref[...] = (_sigmoid(gsb_ref[...]) * a + _sigmoid(gdn_ref[...]) * b).astype(o_ref.dtype)


def _merge(o_sb, o_dn, w_sb, w_dn, gates, *, tm, tn, name):
    n = o_sb.shape[0]
    w = BRANCH_WIDTH
    nj = D_MODEL // tn
    return pl.pallas_call(
        _merge_kernel,
        out_shape=jax.ShapeDtypeStruct((n, D_MODEL), BF16),
        grid=(n // tm, nj),
        in_specs=[pl.BlockSpec((tm, w), lambda i, j: (i, 0)),
                  pl.BlockSpec((tm, w), lambda i, j: (i, 0)),
                  pl.BlockSpec((w, tn), lambda i, j: (0, j)),
                  pl.BlockSpec((w, tn), lambda i, j: (0, j)),
                  pl.BlockSpec((tm, tn), lambda i, j: (i, j)),
                  pl.BlockSpec((tm, tn), lambda i, j: (i, j + nj))],
        out_specs=pl.BlockSpec((tm, tn), lambda i, j: (i, j)),
        compiler_params=_params("parallel", "arbitrary"),
        name=name,
    )(o_sb, o_dn, w_sb, w_dn, gates, gates)


def _out_ln_kernel(m_ref, w_ref, xn_ref, ga_ref, sc_ref, sh_ref, g_ref, b_ref, wr_ref, br_ref,
                   x1_ref, h_ref, lg_ref):
    y = _dot(m_ref[...], w_ref[...])
    x1 = _layer_norm(DEEPNORM_ALPHA * xn_ref[...] + ga_ref[...] * y, g_ref[...], b_ref[...])
    x1_ref[...] = x1
    h = x1 * (1.0 + sc_ref[...]) + sh_ref[...]
    h_ref[...] = h.astype(BF16)
    lg_ref[...] = _dot3(h, wr_ref[...]) + br_ref[...]


def _out_ln(merged, w_out, xn, mod, ln_g, ln_b, w_router, b_router, *, tm, name):
    n = merged.shape[0]
    row = lambda i: i
    blk = pl.BlockSpec((tm, D_MODEL), lambda i: (i, 0))
    vec = pl.BlockSpec((1, D_MODEL), lambda i: (0, 0))
    return pl.pallas_call(
        _out_ln_kernel,
        out_shape=(jax.ShapeDtypeStruct((n, D_MODEL), F32), jax.ShapeDtypeStruct((n, D_MODEL), BF16),
                   jax.ShapeDtypeStruct((n, LANES), F32)),
        grid=(n // tm,),
        in_specs=[blk, pl.BlockSpec((D_MODEL, D_MODEL), lambda i: (0, 0)), blk,
                  mod.spec(2, tm, row), mod.spec(4, tm, row), mod.spec(3, tm, row), vec, vec,
                  pl.BlockSpec((D_MODEL, LANES), lambda i: (0, 0)),
                  pl.BlockSpec((1, LANES), lambda i: (0, 0))],
        out_specs=(blk, blk, pl.BlockSpec((tm, LANES), lambda i: (i, 0))),
        compiler_params=_params("parallel"),
        name=name,
    )(merged, w_out, xn, mod.arr, mod.arr, mod.arr, ln_g, ln_b, w_router, b_router)


def _moe_up_kernel(be_ref, na_ref, x_ref, wg_ref, wu_ref, bg_ref, bu_ref, o_ref, wg_bf, wu_bf):
    m = pl.program_id(1)
    prev = be_ref[jnp.maximum(m - 1, 0)]
    new_weights = jnp.logical_or(m == 0, be_ref[m] != prev)

    @pl.when(m < na_ref[0])
    def _():
        @pl.when(new_weights)
        def _():
            wg_bf[...] = wg_ref[...].astype(BF16)
            wu_bf[...] = wu_ref[...].astype(BF16)

        x = x_ref[...]
        gate = jnp.minimum(_dot(x, wg_bf[...]) + bg_ref[...], SWIGLU_LIMIT)
        up = jnp.clip(_dot(x, wu_bf[...]) + bu_ref[...], -SWIGLU_LIMIT, SWIGLU_LIMIT)
        o_ref[...] = ((up + 1.0) * gate * _sigmoid(gate * SWIGLU_ALPHA)).astype(o_ref.dtype)

    @pl.when(m >= na_ref[0])
    def _():
        o_ref[...] = jnp.zeros_like(o_ref)


def _moe_down_kernel(be_ref, na_ref, a_ref, w_ref, b_ref, o_ref, w_bf):
    m = pl.program_id(1)
    prev = be_ref[jnp.maximum(m - 1, 0)]
    new_weights = jnp.logical_or(m == 0, be_ref[m] != prev)

    @pl.when(m < na_ref[0])
    def _():
        @pl.when(new_weights)
        def _():
            w_bf[...] = w_ref[...].astype(BF16)

        o_ref[...] = (_dot(a_ref[...], w_bf[...]) + b_ref[...]).astype(o_ref.dtype)

    @pl.when(m >= na_ref[0])
    def _():
        o_ref[...] = jnp.zeros_like(o_ref)


def _moe_experts(xs, block_e, n_active, w_gu, b_gu, w_dn, b_dn):
    n_rows = xs.shape[0]
    nb = n_rows // MOE_TM
    nc = D_EXPERT // MOE_TH
    b_gu3 = b_gu.reshape(N_EXPERTS, 1, 2 * D_EXPERT)
    b_dn3 = b_dn.reshape(N_EXPERTS, 1, D_MODEL)
    act = pl.pallas_call(
        _moe_up_kernel,
        out_shape=jax.ShapeDtypeStruct((n_rows, D_EXPERT), BF16),
        grid_spec=pltpu.PrefetchScalarGridSpec(
            num_scalar_prefetch=2,
            grid=(nc, nb),
            in_specs=[pl.BlockSpec((MOE_TM, D_MODEL), lambda c, m, be, na: (m, 0)),
                      pl.BlockSpec((None, D_MODEL, MOE_TH), lambda c, m, be, na: (be[m], 0, c)),
                      pl.BlockSpec((None, D_MODEL, MOE_TH), lambda c, m, be, na: (be[m], 0, nc + c)),
                      pl.BlockSpec((None, 1, MOE_TH), lambda c, m, be, na: (be[m], 0, c)),
                      pl.BlockSpec((None, 1, MOE_TH), lambda c, m, be, na: (be[m], 0, nc + c))],
            out_specs=pl.BlockSpec((MOE_TM, MOE_TH), lambda c, m, be, na: (m, c)),
            scratch_shapes=[pltpu.VMEM((D_MODEL, MOE_TH), BF16), pltpu.VMEM((D_MODEL, MOE_TH), BF16)],
        ),
        compiler_params=_params("arbitrary", "arbitrary"),
        name="moe_up",
    )(block_e, n_active, xs, w_gu, w_gu, b_gu3, b_gu3)
    nj = D_MODEL // MOE_TN
    return pl.pallas_call(
        _moe_down_kernel,
        out_shape=jax.ShapeDtypeStruct((n_rows, D_MODEL), BF16),
        grid_spec=pltpu.PrefetchScalarGridSpec(
            num_scalar_prefetch=2,
            grid=(nj, nb),
            in_specs=[pl.BlockSpec((MOE_TM, D_EXPERT), lambda j, m, be, na: (m, 0)),
                      pl.BlockSpec((None, D_EXPERT, MOE_TN), lambda j, m, be, na: (be[m], 0, j)),
                      pl.BlockSpec((None, 1, MOE_TN), lambda j, m, be, na: (be[m], 0, j))],
            out_specs=pl.BlockSpec((MOE_TM, MOE_TN), lambda j, m, be, na: (m, j)),
            scratch_shapes=[pltpu.VMEM((D_EXPERT, MOE_TN), BF16)],
        ),
        compiler_params=_params("arbitrary", "arbitrary"),
        name="moe_down",
    )(block_e, n_active, act, w_dn, b_dn3)


def _rank_kernel(e_ref, rank_ref, cnt_ref, run_ref, *, rb):
    @pl.when(pl.program_id(0) == 0)
    def _():
        run_ref[...] = jnp.zeros_like(run_ref)

    expert = lax.broadcasted_iota(jnp.int32, (N_EXPERTS, rb), 0)
    onehot = expert == e_ref[...]
    r = lax.broadcasted_iota(jnp.int32, (rb, rb), 0)
    c = lax.broadcasted_iota(jnp.int32, (rb, rb), 1)
    before = _dot(jnp.where(onehot, 1.0, 0.0).astype(BF16), (r < c).astype(BF16))
    run = run_ref[...]
    rank = jnp.sum(jnp.where(onehot, before + run, 0.0), axis=0, keepdims=True)
    rank_ref[...] = rank.astype(jnp.int32)
    run = run + jnp.sum(jnp.where(onehot, 1.0, 0.0), axis=1, keepdims=True)
    run_ref[...] = run
    cnt_ref[...] = jnp.broadcast_to(run, cnt_ref.shape)


def _expert_ranks(flat_e, rb=ROUTE_BLOCK):
    n = flat_e.shape[0]
    flat_e = jnp.pad(flat_e, (0, -n % rb), constant_values=N_EXPERTS)
    nb = flat_e.shape[0] // rb
    rank, cnt = pl.pallas_call(
        functools.partial(_rank_kernel, rb=rb),
        out_shape=(jax.ShapeDtypeStruct((nb, 1, rb), jnp.int32),
                   jax.ShapeDtypeStruct((N_EXPERTS, LANES), F32)),
        grid=(nb,),
        in_specs=[pl.BlockSpec((None, 1, rb), lambda i: (i, 0, 0))],
        out_specs=(pl.BlockSpec((None, 1, rb), lambda i: (i, 0, 0)),
                   pl.BlockSpec((N_EXPERTS, LANES), lambda i: (0, 0))),
        scratch_shapes=[pltpu.VMEM((N_EXPERTS, 1), F32)],
        compiler_params=_params("arbitrary"),
        name="expert_ranks",
    )(flat_e.reshape(nb, 1, rb))
    return rank.reshape(-1)[:n], cnt[:, 0].astype(jnp.int32)


def _route(logits, tm):
    n = logits.shape[0]
    n_assign = n * TOP_K
    n_blocks = -(-n_assign // tm) + N_EXPERTS
    top_logit, top_e = lax.top_k(logits, TOP_K)
    gates = jax.nn.softmax(top_logit, axis=-1).reshape(-1)
    flat_e = top_e.reshape(-1).astype(jnp.int32)
    rank, counts = _expert_ranks(flat_e)
    padded = (counts + tm - 1) // tm * tm
    pad_end = jnp.cumsum(padded)
    pad_start = pad_end - padded
    start = jnp.cumsum(counts) - counts
    onehot = flat_e[:, None] == jnp.arange(N_EXPERTS, dtype=jnp.int32)[None, :]
    slot_of = jnp.sum(jnp.where(onehot, pad_start[None, :], 0), axis=1) + rank
    order = jnp.argsort(flat_e, stable=True).astype(jnp.int32)
    blk0 = jnp.arange(n_blocks, dtype=jnp.int32) * tm
    block_e = jnp.minimum(jnp.sum((pad_end[None, :] <= blk0[:, None]).astype(jnp.int32), axis=1), N_EXPERTS - 1)
    base_rank = blk0 - pad_start[block_e]
    n_valid = jnp.clip(counts[block_e] - base_rank, 0, tm)
    src0 = start[block_e] + base_rank
    r = jnp.arange(tm, dtype=jnp.int32)[None, :]
    valid = r < n_valid[:, None]
    src = order[jnp.clip(src0[:, None] + r, 0, n_assign - 1)]
    slot_tok = jnp.where(valid, src // TOP_K, (blk0[:, None] + r) % n).reshape(-1)
    n_active = (pad_end[-1] // tm).astype(jnp.int32).reshape(1)
    return slot_tok, block_e, n_active, slot_of.reshape(n, TOP_K), gates.reshape(n, TOP_K)


def _combine_ln_kernel(x1_ref, y_ref, gt_ref, ga_ref, g_ref, b_ref, o_ref):
    gates = gt_ref[...]
    ffn = y_ref[0].astype(F32) * gates[:, 0:1]
    for k in range(1, TOP_K):
        ffn = ffn + y_ref[k].astype(F32) * gates[:, k:k + 1]
    o_ref[...] = _layer_norm(DEEPNORM_ALPHA * x1_ref[...] + ga_ref[...] * ffn, g_ref[...], b_ref[...])


def _combine_ln(x1, y4, gates, row0, mod, ln_g, ln_b, *, tm, name):
    n = x1.shape[0]
    rb = row0 // tm
    row = lambda i: i
    blk = pl.BlockSpec((tm, D_MODEL), lambda i: (i, 0))
    vec = pl.BlockSpec((1, D_MODEL), lambda i: (0, 0))
    return pl.pallas_call(
        _combine_ln_kernel,
        out_shape=jax.ShapeDtypeStruct((n, D_MODEL), F32),
        grid=(n // tm,),
        in_specs=[blk, pl.BlockSpec((TOP_K, tm, D_MODEL), lambda i: (0, i + rb, 0)),
                  pl.BlockSpec((tm, TOP_K), lambda i: (i + rb, 0)),
                  mod.spec(5, tm, row), vec, vec],
        out_specs=blk,
        compiler_params=_params("parallel"),
        name=name,
    )(x1, y4, gates, mod.arr, ln_g, ln_b)


def _dense_front(x, mod, w, ln_in_g, ln_in_b, *, tm, tag):
    xn, h = _ln_mod(x, ln_in_g, ln_in_b, mod, tm=tm, name=f"ln_mod_{tag}")
    mm = functools.partial(_matmul, h, w["w_in_main"], tm=tm)
    q = mm(tn=1024, out_dtype=F32, n_out=BRANCH_WIDTH, col0=0, name=f"proj_q_{tag}")
    k = mm(tn=1024, out_dtype=F32, n_out=BRANCH_WIDTH, col0=BRANCH_WIDTH, name=f"proj_k_{tag}")
    v = mm(tn=1024, out_dtype=F32, n_out=BRANCH_WIDTH, col0=2 * BRANCH_WIDTH, name=f"proj_v_{tag}")
    dnz = mm(tn=1024, out_dtype=F32, n_out=4 * BRANCH_WIDTH, col0=3 * BRANCH_WIDTH, name=f"proj_dn_{tag}")
    mt = functools.partial(_matmul, h, w["w_in_tail"], tm=tm)
    gates = mt(tn=1024, out_dtype=F32, n_out=2 * D_MODEL, col0=0, name=f"proj_gate_{tag}")
    ba = mt(tn=LANES, out_dtype=F32, n_out=LANES, col0=2 * D_MODEL, name=f"proj_ba_{tag}")
    return xn, q, k, v, dnz, gates, ba


def _dense_back(o_sb, o_dn, gates, xn, mod, w, *, tm, tm_out, tag):
    merged = _merge(o_sb, o_dn, w["w_branch_sb"], w["w_branch_dn"], gates, tm=tm, tn=1024, name=f"merge_{tag}")
    return _out_ln(merged, w["w_out"], xn, mod, w["ln1_g"], w["ln1_b"], w["w_router"], w["b_router"],
                   tm=tm_out, name=f"out_ln_{tag}")


def _logit_rows(ba, n_seq, seq, chunk):
    t = ba[:, :2 * N_HEADS].reshape(n_seq * seq // chunk, chunk, 2 * N_HEADS)
    return t.transpose(0, 2, 1)


def kernel(x_prompt, x_sample, cache_k, cache_v, state_dn, state_conv, page_table, c_prompt, c_sample,
           ln_in_g, ln_in_b, w_ada, b_ada, w_in, sb_bias, conv_w, dn_a_log, dn_dt_bias, dn_norm_w,
           w_branch_sb, w_branch_dn, w_out, ln1_g, ln1_b, w_router, b_router, w_gu, b_gu,
           w_dn, b_dn, ln2_g, ln2_b):
    assert w_ada.shape[0] == DEPTH == 1
    bp, sp, d = x_prompt.shape
    bs, ss, _ = x_sample.shape
    n_p, n_s = bp * sp, bs * ss
    l = 0

    w_in_l = w_in[l]
    w_in_tail = jnp.concatenate(
        [w_in_l[:, OFF_GATE:], w_in_l[:, OFF_DN_B:OFF_GATE],
         jnp.zeros((d, LANES - 2 * N_HEADS), w_in.dtype)], axis=1).astype(BF16)
    w = {
        "w_in_main": w_in_l, "w_in_tail": w_in_tail,
        "w_branch_sb": w_branch_sb[l].astype(BF16), "w_branch_dn": w_branch_dn[l].astype(BF16),
        "w_out": w_out[l].astype(BF16),
        "ln1_g": ln1_g[l].reshape(1, d), "ln1_b": ln1_b[l].reshape(1, d),
        "w_router": jnp.pad(w_router[l], ((0, 0), (0, LANES - N_EXPERTS))),
        "b_router": jnp.pad(b_router[l], (0, LANES - N_EXPERTS)).reshape(1, LANES),
    }
    ln2g, ln2b = ln2_g[l].reshape(1, d), ln2_b[l].reshape(1, d)
    conv_w8 = jnp.pad(conv_w[l], ((0, SUBLANES - DN_CONV), (0, 0)))
    norm_w = dn_norm_w[l].reshape(1, HEAD_DIM)
    a_scale = -jnp.exp(dn_a_log[l].astype(F32))

    c_all = jnp.concatenate([c_prompt, c_sample], axis=0)
    mod_all = _matmul(c_all, w_ada[l], tm=c_all.shape[0], tn=1024, out_dtype=F32, n_out=6 * d,
                      bias=b_ada[l].reshape(1, 6 * d), silu_in=True, name="ada")
    mod_p = _Mod(mod_all[:bp], sp, n_p)
    mod_s = _Mod(mod_all[bp:], ss, n_s)

    xn_p, q_p, k_p, v_p, dnz_p, gates_p, ba_p = _dense_front(
        x_prompt.reshape(n_p, d), mod_p, w, ln_in_g, ln_in_b, tm=min(1024, n_p), tag="p")
    o_sb_p = _sb_prompt(q_p, k_p, v_p, sb_bias[l], n_seq=bp, seq=sp)
    zero_prev = jnp.zeros((bp, SUBLANES, 3 * BRANCH_WIDTH), F32)
    dq, dk, dv = _dn_prep(dnz_p, zero_prev, conv_w8, n_seq=bp, seq=sp, valid=sp, heads_per_step=1,
                          name="dn_prep_p")
    o_dn_p, s_p = _dn_chunks(dq, dk, dv, dnz_p, 3 * BRANCH_WIDTH, ba_p, _logit_rows(ba_p, bp, sp, DN_CHUNK),
                             a_scale, dn_dt_bias[l], norm_w,
                             jnp.zeros((bp, N_HEADS, HEAD_DIM, HEAD_DIM), F32),
                             n_seq=bp, seq=sp, chunk=DN_CHUNK, rows=4 * DN_CHUNK)
    x1_p, h2_p, lg_p = _dense_back(o_sb_p, o_dn_p, gates_p, xn_p, mod_p, w, tm=min(1024, n_p), tm_out=256,
                                   tag="p")

    xn_s, q_s, k_s, v_s, dnz_s, gates_s, ba_s = _dense_front(
        x_sample.reshape(n_s, d), mod_s, w, ln_in_g, ln_in_b, tm=n_s, tag="s")
    pad_q = SUBLANES - ss

    def heads_rows(t):
        t = t.reshape(bs, ss, N_HEADS, HEAD_DIM).transpose(0, 2, 1, 3)
        return (jnp.pad(t, ((0, 0), (0, 0), (0, SAMPLE_QROWS - ss), (0, 0)))
                .reshape(bs, N_HEADS * SAMPLE_QROWS, HEAD_DIM))

    def new_page(t):
        t = jnp.pad(t.reshape(bs, ss, BRANCH_WIDTH), ((0, 0), (0, PAGE_SIZE - ss), (0, 0)))
        return t.reshape(bs, PAGE_SIZE * N_HEADS, HEAD_DIM)

    n_phys = cache_k.shape[1]
    o_s = _sb_sample(heads_rows(q_s), new_page(k_s), new_page(v_s), sb_bias[l],
                     cache_k.reshape(-1, PAGE_SIZE * N_HEADS, HEAD_DIM),
                     cache_v.reshape(-1, PAGE_SIZE * N_HEADS, HEAD_DIM), page_table, l * n_phys)
    o_sb_s = (o_s.reshape(bs, N_HEADS, SAMPLE_QROWS, HEAD_DIM)[:, :, :ss].transpose(0, 2, 1, 3)
              .reshape(n_s, BRANCH_WIDTH).astype(BF16))

    def pad_rows(t):
        return jnp.pad(t.reshape(bs, ss, -1), ((0, 0), (0, pad_q), (0, 0))).reshape(bs * SUBLANES, -1)

    dnz_s8 = pad_rows(dnz_s)
    ba_s8 = pad_rows(ba_s)
    prev_s = jnp.pad(state_conv[l], ((0, 0), (SUBLANES - (DN_CONV - 1), 0), (0, 0)))
    dq, dk, dv = _dn_prep(dnz_s8, prev_s, conv_w8, n_seq=bs, seq=SUBLANES, valid=ss, heads_per_step=N_HEADS,
                          name="dn_prep_s")
    tok = jnp.arange(bs * SUBLANES) % SUBLANES
    neutral = jnp.concatenate([jnp.full((N_HEADS,), -1e30, F32), jnp.full((N_HEADS,), -1e30, F32),
                               jnp.zeros((LANES - 2 * N_HEADS,), F32)])
    ba_s8 = jnp.where((tok < ss)[:, None], ba_s8, neutral[None, :])
    o_dn_s8, s_s = _dn_chunks(dq, dk, dv, dnz_s8, 3 * BRANCH_WIDTH, ba_s8, _logit_rows(ba_s8, bs, SUBLANES, SUBLANES),
                              a_scale, dn_dt_bias[l], norm_w, state_dn[l],
                              n_seq=bs, seq=SUBLANES, chunk=SUBLANES, rows=SUBLANES)
    o_dn_s = o_dn_s8.reshape(bs, SUBLANES, BRANCH_WIDTH)[:, :ss].reshape(n_s, BRANCH_WIDTH)
    x1_s, h2_s, lg_s = _dense_back(o_sb_s, o_dn_s, gates_s, xn_s, mod_s, w, tm=n_s, tm_out=n_s, tag="s")

    logits = jnp.concatenate([lg_p[:, :N_EXPERTS], lg_s[:, :N_EXPERTS]], axis=0)
    h2 = jnp.concatenate([h2_p, h2_s], axis=0)
    slot_tok, block_e, n_active, slot_of, gates = _route(logits, MOE_TM)
    xs = h2.at[slot_tok].get(mode="promise_in_bounds")
    yb = _moe_experts(xs, block_e, n_active, w_gu[l], b_gu[l], w_dn[l], b_dn[l])
    y4 = yb.at[slot_of.T].get(mode="promise_in_bounds")
    y_p = _combine_ln(x1_p, y4, gates, 0, mod_p, ln2g, ln2b, tm=256, name="combine_ln_p")
    y_s = _combine_ln(x1_s, y4, gates, n_p, mod_s, ln2g, ln2b, tm=n_s, name="combine_ln_s")

    heads = lambda t, b, s: t.reshape(1, b, s, N_HEADS, HEAD_DIM)
    conv_p = dnz_p.reshape(bp, sp, 4 * BRANCH_WIDTH)[:, sp - (DN_CONV - 1):, :3 * BRANCH_WIDTH]
    xp_s = jnp.concatenate([state_conv[l], dnz_s.reshape(bs, ss, 4 * BRANCH_WIDTH)[:, :, :3 * BRANCH_WIDTH]], axis=1)
    conv_s = xp_s[:, -(DN_CONV - 1):]
    return (y_p.reshape(bp, sp, d), y_s.reshape(bs, ss, d),
            heads(k_p, bp, sp), heads(v_p, bp, sp), s_p[None], conv_p[None],
            heads(k_s, bs, ss), heads(v_s, bs, ss), s_s[None], conv_s[None])
```

```python
import functools
import math

import jax
import jax.numpy as jnp
from jax import lax
from jax.experimental import pallas as pl
from jax.experimental.pallas import tpu as pltpu

F32 = jnp.float32
BF16 = jnp.bfloat16

D_MODEL = 2048
HEAD_DIM = 128
N_HEADS = 8
BRANCH_WIDTH = N_HEADS * HEAD_DIM
PAGE_SIZE = 128
DN_CONV = 4
DN_CHUNK = 64
N_EXPERTS = 32
TOP_K = 4
D_EXPERT = D_MODEL
SWIGLU_LIMIT = 7.0
SWIGLU_ALPHA = 1.702
LN_EPS = 1e-5
RMS_EPS = 1e-6
L2_EPS = 1e-6
DEPTH = 1
DEEPNORM_ALPHA = (2.0 * DEPTH) ** 0.25
OFF_DN_QKV = 3 * BRANCH_WIDTH
OFF_DN_Z = OFF_DN_QKV + 3 * BRANCH_WIDTH
OFF_DN_B = OFF_DN_Z + BRANCH_WIDTH
OFF_GATE = OFF_DN_B + 2 * N_HEADS
N_IN = OFF_GATE + 2 * D_MODEL

LANES = 128
SUBLANES = 8
VMEM_LIMIT = 48 * 1024 * 1024
MOE_TM = 512
MOE_TH = 1024
MOE_UP_VMEM = 56 * 1024 * 1024
MOE_TN = 1024
ROUTE_BLOCK = 512
SB_PROMPT_TILE = 256
SB_PROMPT_HEADS = 4
SAMPLE_PAGES_PER_STEP = 16
SAMPLE_QROWS = 16


def _params(*sem, vmem=VMEM_LIMIT):
    return pltpu.CompilerParams(dimension_semantics=sem, vmem_limit_bytes=vmem)


def _dot(a, b):
    return jnp.dot(a, b, preferred_element_type=F32)


def _dot_nt(a, b):
    return lax.dot_general(a, b, (((1,), (1,)), ((), ())), preferred_element_type=F32)


def _dot_tn(a, b):
    return lax.dot_general(a, b, (((0,), (0,)), ((), ())), preferred_element_type=F32)


def _split(x):
    hi = x.astype(BF16)
    lo = (x - hi.astype(F32)).astype(BF16)
    return hi, lo


def _dot3(a, b):
    ah, al = _split(a)
    bh, bl = _split(b)
    return _dot(ah, bh) + (_dot(ah, bl) + _dot(al, bh))


def _sigmoid(x):
    return 1.0 / (1.0 + jnp.exp(-x))


def _softplus(x):
    return jnp.maximum(x, 0.0) + jnp.log1p(jnp.exp(-jnp.abs(x)))


def _layer_norm(x, g, b):
    mu = jnp.mean(x, axis=-1, keepdims=True)
    xc = x - mu
    var = jnp.mean(xc * xc, axis=-1, keepdims=True)
    return xc * lax.rsqrt(var + LN_EPS) * g + b


class _Mod:
    def __init__(self, mod, rows_per_seq, n_rows):
        n_seq = mod.shape[0]
        self.grouped = rows_per_seq % SUBLANES == 0 and rows_per_seq >= LANES
        self.rows_per_seq = rows_per_seq
        m6 = mod.reshape(n_seq, 6, D_MODEL).transpose(1, 0, 2)
        if self.grouped:
            self.arr = m6.reshape(6, n_seq, 1, D_MODEL)
        else:
            self.arr = jnp.repeat(m6, rows_per_seq, axis=1)
            assert self.arr.shape[1] == n_rows

    def spec(self, comp, tm, row_block_of):
        if self.grouped:
            bps = self.rows_per_seq // tm
            return pl.BlockSpec((None, None, 1, D_MODEL),
                                lambda *g: (comp, row_block_of(*g) // bps, 0, 0))
        return pl.BlockSpec((None, tm, D_MODEL), lambda *g: (comp, row_block_of(*g), 0))


def _mm_kernel(x_ref, w_ref, *rest, has_bias, silu_in):
    if has_bias:
        b_ref, o_ref = rest
    else:
        (o_ref,) = rest
    x = x_ref[...]
    if silu_in:
        x = x * _sigmoid(x)
    acc = _dot(x.astype(BF16), w_ref[...].astype(BF16))
    if has_bias:
        acc = acc + b_ref[...]
    o_ref[...] = acc.astype(o_ref.dtype)


def _matmul(x, w, *, tm, tn, out_dtype, n_out, col0=0, bias=None, silu_in=False, name):
    n, k = x.shape
    assert n % tm == 0 and n_out % tn == 0 and col0 % tn == 0
    cb = col0 // tn
    in_specs = [pl.BlockSpec((tm, k), lambda i, j: (i, 0)),
                pl.BlockSpec((k, tn), lambda i, j: (0, j + cb))]
    args = [x, w]
    if bias is not None:
        in_specs.append(pl.BlockSpec((1, tn), lambda i, j: (0, j + cb)))
        args.append(bias)
    return pl.pallas_call(
        functools.partial(_mm_kernel, has_bias=bias is not None, silu_in=silu_in),
        out_shape=jax.ShapeDtypeStruct((n, n_out), out_dtype),
        grid=(n // tm, n_out // tn),
        in_specs=in_specs,
        out_specs=pl.BlockSpec((tm, tn), lambda i, j: (i, j)),
        compiler_params=_params("parallel", "arbitrary"),
        name=name,
    )(*args)


def _ln_mod_kernel(x_ref, g_ref, b_ref, sc_ref, sh_ref, xn_ref, h_ref):
    xn = _layer_norm(x_ref[...], g_ref[...], b_ref[...])
    xn_ref[...] = xn
    h_ref[...] = (xn * (1.0 + sc_ref[...]) + sh_ref[...]).astype(BF16)


def _ln_mod(x, g, b, mod, *, tm, name):
    n = x.shape[0]
    row = lambda i: i
    vec = pl.BlockSpec((1, D_MODEL), lambda i: (0, 0))
    blk = pl.BlockSpec((tm, D_MODEL), lambda i: (i, 0))
    return pl.pallas_call(
        _ln_mod_kernel,
        out_shape=(jax.ShapeDtypeStruct((n, D_MODEL), F32), jax.ShapeDtypeStruct((n, D_MODEL), BF16)),
        grid=(n // tm,),
        in_specs=[blk, vec, vec, mod.spec(1, tm, row), mod.spec(0, tm, row)],
        out_specs=(blk, blk),
        compiler_params=_params("parallel"),
        name=name,
    )(x, g.reshape(1, D_MODEL), b.reshape(1, D_MODEL), mod.arr, mod.arr)


def _stick_blocks(qs, kbs, vbs, biases, log_afters, later_mat, mask):
    n = range(len(qs))
    zs = [_dot_nt(qs[i], kbs[i]) + biases[i] for i in n]
    log_keep, log_beta = [], []
    for z in zs:
        t = jnp.log(1.0 + jnp.exp(-jnp.abs(z)))
        lk = -(jnp.maximum(z, 0.0) + t)
        log_keep.append(lk if mask is None else jnp.where(mask, lk, 0.0))
        log_beta.append(jnp.minimum(z, 0.0) - t)
    parts = [_split(lk) for lk in log_keep]
    within = [_dot(hi, later_mat) + _dot(lo, later_mat) for hi, lo in parts]
    ws = []
    for i in n:
        w = jnp.exp(log_beta[i] + (within[i] + log_afters[i]))
        ws.append((w if mask is None else jnp.where(mask, w, 0.0)).astype(BF16))
    outs = [_dot(ws[i], vbs[i]) for i in n]
    new_after = [log_afters[i] + jnp.sum(log_keep[i], axis=-1, keepdims=True) for i in n]
    return outs, new_after


def _later_matrix(tk):
    r = lax.broadcasted_iota(jnp.int32, (tk, tk), 0)
    c = lax.broadcasted_iota(jnp.int32, (tk, tk), 1)
    return (r > c).astype(BF16)


def _sb_prompt_kernel(bias_ref, q_ref, k_ref, v_ref, o_ref, *, t, hp, scale):
    hg = pl.program_id(1)
    qi = pl.program_id(2)
    heads = range(hp)
    hcols = [slice(i * HEAD_DIM, (i + 1) * HEAD_DIM) for i in heads]
    biases = [bias_ref[hg * hp + i] for i in heads]
    qs = [(q_ref[:, hc] * scale).astype(BF16) for hc in hcols]
    later_mat = _later_matrix(t)
    r = lax.broadcasted_iota(jnp.int32, (t, t), 0)
    c = lax.broadcasted_iota(jnp.int32, (t, t), 1)

    def kv(j):
        rows = pl.ds(pl.multiple_of(j * t, t), t)
        return ([k_ref[rows, hc].astype(BF16) for hc in hcols],
                [v_ref[rows, hc].astype(BF16) for hc in hcols])

    kbs, vbs = kv(qi)
    zero = [jnp.zeros((t, 1), F32) for _ in heads]
    accs, afters = _stick_blocks(qs, kbs, vbs, biases, zero, later_mat, c < r)

    def body(step, carry):
        accs, afters = carry
        kbs, vbs = kv(qi - 1 - step)
        outs, afters = _stick_blocks(qs, kbs, vbs, biases, list(afters), later_mat, None)
        return tuple(a + o for a, o in zip(accs, outs)), tuple(afters)

    accs, _ = lax.fori_loop(0, qi, body, (tuple(accs), tuple(afters)))
    for i in heads:
        o_ref[:, hcols[i]] = accs[i].astype(o_ref.dtype)


def _sb_prompt(q, k, v, bias, *, n_seq, seq, t=SB_PROMPT_TILE, hp=SB_PROMPT_HEADS):
    nq = seq // t
    w = hp * HEAD_DIM
    return pl.pallas_call(
        functools.partial(_sb_prompt_kernel, t=t, hp=hp, scale=HEAD_DIM ** -0.5),
        out_shape=jax.ShapeDtypeStruct(q.shape, BF16),
        grid=(n_seq, N_HEADS // hp, nq),
        in_specs=[pl.BlockSpec(memory_space=pltpu.SMEM),
                  pl.BlockSpec((t, w), lambda b, h, i: (b * nq + i, h)),
                  pl.BlockSpec((seq, w), lambda b, h, i: (b, h)),
                  pl.BlockSpec((seq, w), lambda b, h, i: (b, h))],
        out_specs=pl.BlockSpec((t, w), lambda b, h, i: (b * nq + i, h)),
        compiler_params=_params("parallel", "parallel", "arbitrary"),
        name="sb_prompt",
    )(bias, q, k, v)


def _sb_sample_kernel(pt_ref, bias_ref, q_ref, kn_ref, vn_ref, *rest, n_pg, scale):
    k_refs = rest[:n_pg]
    v_refs = rest[n_pg:2 * n_pg]
    o_ref = rest[2 * n_pg]
    acc_ref, la_ref = rest[2 * n_pg + 1:]
    step = pl.program_id(1)
    qr = SAMPLE_QROWS
    rows = N_HEADS * qr
    later_mat = _later_matrix(PAGE_SIZE)
    bias = jnp.concatenate(
        [jnp.full((qr, 1), bias_ref[h], F32) for h in range(N_HEADS)], axis=0)
    q = (q_ref[...] * scale).astype(BF16)

    def head_rows(page_ref, h):
        return page_ref[pl.ds(h, PAGE_SIZE, stride=N_HEADS), :].astype(BF16)

    def pages(k_list, v_list, mask):
        n = len(k_list)
        hrows = [slice(h * qr, (h + 1) * qr) for h in range(N_HEADS)]
        page_lanes = [slice(i * PAGE_SIZE, (i + 1) * PAGE_SIZE) for i in range(n)]
        zs = []
        for h in range(N_HEADS):
            k_cat = jnp.concatenate([head_rows(k_ref, h) for k_ref in k_list], axis=0)
            zs.append(_dot_nt(q[hrows[h]], k_cat))
        z = jnp.concatenate(zs, axis=0) + bias
        t = jnp.log(1.0 + jnp.exp(-jnp.abs(z)))
        log_keep = -(jnp.maximum(z, 0.0) + t)
        log_beta = jnp.minimum(z, 0.0) - t
        if mask is not None:
            log_keep = jnp.where(mask, log_keep, 0.0)
        hi, lo = _split(jnp.concatenate([log_keep[:, pl_] for pl_ in page_lanes], axis=0))
        within = _dot(hi, later_mat) + _dot(lo, later_mat)
        la = la_ref[...]
        later = []
        for i in range(n):
            later.append(within[i * rows:(i + 1) * rows] + la)
            la = la + jnp.sum(log_keep[:, page_lanes[i]], axis=-1, keepdims=True)
        la_ref[...] = la
        w = jnp.exp(log_beta + jnp.concatenate(later, axis=1))
        if mask is not None:
            w = jnp.where(mask, w, 0.0)
        w = w.astype(BF16)
        for h in range(N_HEADS):
            v_cat = jnp.concatenate([head_rows(v_ref, h) for v_ref in v_list], axis=0)
            acc_ref[hrows[h], :] += _dot(w[hrows[h]], v_cat)

    @pl.when(step == 0)
    def _():
        acc_ref[...] = jnp.zeros_like(acc_ref)
        la_ref[...] = jnp.zeros_like(la_ref)
        r = lax.broadcasted_iota(jnp.int32, (rows, PAGE_SIZE), 0) % qr
        c = lax.broadcasted_iota(jnp.int32, (rows, PAGE_SIZE), 1)
        pages([kn_ref], [vn_ref], c < r)

    @pl.when(step > 0)
    def _():
        pages(list(k_refs), list(v_refs), None)

    @pl.when(step == pl.num_programs(1) - 1)
    def _():
        o_ref[...] = acc_ref[...]


def _sb_sample(q, k_new, v_new, bias, cache_k, cache_v, page_table, page0):
    bd, n_pages = page_table.shape
    n_pg = SAMPLE_PAGES_PER_STEP
    assert n_pages % n_pg == 0
    n_steps = n_pages // n_pg
    page_rows = PAGE_SIZE * N_HEADS
    rows = N_HEADS * SAMPLE_QROWS

    def page_spec(i):
        def index(b, s, pt):
            logical = n_pages - 1 - (jnp.maximum(s - 1, 0) * n_pg + i)
            return (pt[b, logical] + page0, 0, 0)
        return pl.BlockSpec((None, page_rows, HEAD_DIM), index)

    new_spec = pl.BlockSpec((None, page_rows, HEAD_DIM), lambda b, s, pt: (b, 0, 0))
    q_spec = pl.BlockSpec((None, rows, HEAD_DIM), lambda b, s, pt: (b, 0, 0))
    grid_spec = pltpu.PrefetchScalarGridSpec(
        num_scalar_prefetch=1,
        grid=(bd, n_steps + 1),
        in_specs=[pl.BlockSpec(memory_space=pltpu.SMEM), q_spec, new_spec, new_spec]
        + [page_spec(i) for i in range(n_pg)] * 2,
        out_specs=q_spec,
        scratch_shapes=[pltpu.VMEM((rows, HEAD_DIM), F32), pltpu.VMEM((rows, 1), F32)],
    )
    return pl.pallas_call(
        functools.partial(_sb_sample_kernel, n_pg=n_pg, scale=HEAD_DIM ** -0.5),
        out_shape=jax.ShapeDtypeStruct((bd, rows, HEAD_DIM), F32),
        grid_spec=grid_spec,
        compiler_params=_params("parallel", "arbitrary"),
        name="sb_sample",
    )(page_table, bias, q, k_new, v_new, *([cache_k] * n_pg), *([cache_v] * n_pg))


def _dn_prep_kernel(xq_ref, xk_ref, xv_ref, pq_ref, pk_ref, pv_ref, wq_ref, wk_ref, wv_ref,
                    oq_ref, ok_ref, ov_ref, *, seq, valid):
    width = xq_ref.shape[1]
    rows8 = lax.broadcasted_iota(jnp.int32, (SUBLANES, width), 0)

    def conv_silu(x_ref, p_ref, w_ref):
        w = w_ref[...]
        x8 = x_ref[0:SUBLANES, :]
        p8 = p_ref[...]
        head = x8 * w[DN_CONV - 1:DN_CONV]
        for k in range(1, DN_CONV):
            shifted = jnp.where(rows8 < k, pltpu.roll(p8, k, 0), pltpu.roll(x8, k, 0))
            head = head + shifted * w[DN_CONV - 1 - k:DN_CONV - k]
        if seq == SUBLANES:
            acc = head
        else:
            x = x_ref[...]
            acc = x * w[DN_CONV - 1:DN_CONV]
            for k in range(1, DN_CONV):
                acc = acc + pltpu.roll(x, k, 0) * w[DN_CONV - 1 - k:DN_CONV - k]
            acc = jnp.concatenate([head, acc[SUBLANES:]], axis=0)
        y = acc * _sigmoid(acc)
        if valid < seq:
            r = lax.broadcasted_iota(jnp.int32, y.shape, 0)
            y = jnp.where(r < valid, y, 0.0)
        return y

    def l2n(y):
        parts = [y[:, i:i + HEAD_DIM] for i in range(0, width, HEAD_DIM)]
        parts = [p * lax.rsqrt(jnp.sum(p * p, axis=-1, keepdims=True) + L2_EPS) for p in parts]
        return parts[0] if len(parts) == 1 else jnp.concatenate(parts, axis=1)

    oq_ref[...] = l2n(conv_silu(xq_ref, pq_ref, wq_ref)) * (HEAD_DIM ** -0.5)
    ok_ref[...] = l2n(conv_silu(xk_ref, pk_ref, wk_ref))
    ov_ref[...] = conv_silu(xv_ref, pv_ref, wv_ref)


def _dn_prep(x, prev, conv_w, *, n_seq, seq, valid, heads_per_step, name):
    nh = N_HEADS // heads_per_step
    wd = heads_per_step * HEAD_DIM
    xs = [pl.BlockSpec((seq, wd), lambda b, h, s=s: (b, s * nh + h)) for s in range(3)]
    ps = [pl.BlockSpec((None, SUBLANES, wd), lambda b, h, s=s: (b, 0, s * nh + h)) for s in range(3)]
    ws = [pl.BlockSpec((SUBLANES, wd), lambda b, h, s=s: (0, s * nh + h)) for s in range(3)]
    out = pl.BlockSpec((seq, wd), lambda b, h: (b, h))
    shape = jax.ShapeDtypeStruct((n_seq * seq, BRANCH_WIDTH), F32)
    return pl.pallas_call(
        functools.partial(_dn_prep_kernel, seq=seq, valid=valid),
        out_shape=(shape, shape, shape),
        grid=(n_seq, nh),
        in_specs=xs + ps + ws,
        out_specs=(out, out, out),
        compiler_params=_params("parallel", "parallel"),
        name=name,
    )(x, x, x, prev, prev, prev, conv_w, conv_w, conv_w)


def _mm3(x, y):
    (xh, xl), (yh, yl) = x, y
    return _dot(xh, yh) + (_dot(xh, yl) + _dot(xl, yh))


def _mm3_shared(lefts, right):
    rows = [x[0].shape[0] for x in lefts]
    if any(n % (2 * SUBLANES) for n in rows):
        return [_mm3(x, right) for x in lefts]
    yh, yl = right
    a = _dot(jnp.concatenate([part for x in lefts for part in x], axis=0), yh)
    b = _dot(jnp.concatenate([x[0] for x in lefts], axis=0), yl)
    out, ra, rb = [], 0, 0
    for n in rows:
        out.append(a[ra:ra + n] + (b[rb:rb + n] + a[ra + n:ra + 2 * n]))
        ra += 2 * n
        rb += n
    return out


def _unit_lower_inverse_each(mats, c):
    r = lax.broadcasted_iota(jnp.int32, (c, c), 0)
    col = lax.broadcasted_iota(jnp.int32, (c, c), 1)
    eye = jnp.where(r == col, 1.0, 0.0)
    p = [eye - a for a in mats]
    a_s = [_split(a) for a in mats]
    power = [_mm3_shared([s], s)[0] for s in a_s]
    for _ in range(int(math.log2(c)) - 2):
        pw_s = [_split(x) for x in power]
        p_s = [_split(x) for x in p]
        both = [_mm3_shared([ws, ps], ws) for ps, ws in zip(p_s, pw_s)]
        power = [b[0] for b in both]
        p = [x + b[1] for x, b in zip(p, both)]
    pw_s = [_split(x) for x in power]
    p_s = [_split(x) for x in p]
    return [x + _mm3_shared([ps], ws)[0] for x, ps, ws in zip(p, p_s, pw_s)]


def _dn_chunk_kernel(ascale_ref, dtb_ref, q_ref, k_ref, v_ref, z_ref, ba_ref, bat_ref, nw_ref, s0_ref,
                     o_ref, s_ref, *, chunk, n_chunks):
    c = chunk
    heads = range(N_HEADS)

    @pl.when(pl.program_id(1) == 0)
    def _():
        s_ref[...] = s0_ref[...]

    r = lax.broadcasted_iota(jnp.int32, (c, c), 0)
    col = lax.broadcasted_iota(jnp.int32, (c, c), 1)
    tri = r >= col
    strict = r > col
    nw = nw_ref[...]

    def one_chunk(ci, carry):
        rows = pl.ds(pl.multiple_of(ci * c, c), c)
        bat = bat_ref[ci]
        hcols = [slice(h * HEAD_DIM, (h + 1) * HEAD_DIM) for h in heads]
        q = [q_ref[rows, hc] for hc in hcols]
        k = [k_ref[rows, hc] for hc in hcols]
        v = [v_ref[rows, hc] for hc in hcols]
        kb = [x.astype(BF16) for x in k]
        if c % (2 * SUBLANES) == 0:
            both = [_dot_nt(jnp.concatenate([kb[h], q[h].astype(BF16)], axis=0), kb[h]) for h in heads]
            kk = [x[:c] for x in both]
            qk = [x[c:] for x in both]
        else:
            kk = [_dot_nt(x, x) for x in kb]
            qk = [_dot_nt(x.astype(BF16), y) for x, y in zip(q, kb)]
        beta, gc_col, g_last, decay, e_col = [], [], [], [], []
        for h in heads:
            a_scale = ascale_ref[h]
            beta.append(_sigmoid(ba_ref[rows, h:h + 1]))
            g_col = a_scale * _softplus(ba_ref[rows, N_HEADS + h:N_HEADS + h + 1] + dtb_ref[h])
            g_row = a_scale * _softplus(bat[N_HEADS + h:N_HEADS + h + 1, :] + dtb_ref[h])
            gcc = jnp.sum(jnp.where(tri, jnp.broadcast_to(g_row, (c, c)), 0.0), axis=1, keepdims=True)
            gcr = jnp.sum(jnp.where(r <= col, jnp.broadcast_to(g_col, (c, c)), 0.0), axis=0, keepdims=True)
            gc_col.append(gcc)
            g_last.append(jnp.sum(g_row, axis=1, keepdims=True))
            decay.append(jnp.exp(jnp.where(tri, gcc - gcr, -1e30)))
            e_col.append(jnp.exp(gcc))
        a_low = [jnp.where(strict, kk[h] * decay[h], 0.0) * beta[h] for h in heads]
        t_inv = _unit_lower_inverse_each(a_low, c)
        rhs = [jnp.concatenate([v[h] * beta[h], k[h] * (beta[h] * e_col[h])], axis=1) for h in heads]
        t_s = [_split(x) for x in t_inv]
        rhs_s = [_split(x) for x in rhs]
        sol = [_mm3_shared([t_s[h]], rhs_s[h])[0] for h in heads]
        state = [s_ref[h] for h in heads]
        sb = [x.astype(BF16) for x in state]
        if c % (2 * SUBLANES) == 0:
            ws = [_dot(jnp.concatenate([sol[h][:, HEAD_DIM:].astype(BF16), (q[h] * e_col[h]).astype(BF16)], axis=0),
                       sb[h]) for h in heads]
            u = [sol[h][:, :HEAD_DIM] - ws[h][:c] for h in heads]
            o_state = [ws[h][c:] for h in heads]
        else:
            u = [sol[h][:, :HEAD_DIM] - _dot(sol[h][:, HEAD_DIM:].astype(BF16), sb[h]) for h in heads]
            o_state = [_dot((q[h] * e_col[h]).astype(BF16), sb[h]) for h in heads]
        ub = [x.astype(BF16) for x in u]
        o = [o_state[h] + _dot(jnp.where(tri, qk[h] * decay[h], 0.0).astype(BF16), ub[h]) for h in heads]
        k_end = [(k[h] * jnp.exp(g_last[h] - gc_col[h])).astype(BF16) for h in heads]
        s_new = [state[h] * jnp.exp(g_last[h]) + _dot_tn(k_end[h], ub[h]) for h in heads]
        for h in heads:
            s_ref[h] = s_new[h]
            on = o[h] * lax.rsqrt(jnp.mean(o[h] * o[h], axis=-1, keepdims=True) + RMS_EPS)
            zz = z_ref[rows, hcols[h]]
            o_ref[rows, hcols[h]] = (on * nw * (zz * _sigmoid(zz))).astype(o_ref.dtype)
        return carry

    lax.fori_loop(0, n_chunks, one_chunk, 0)


def _dn_chunks(q, k, v, z_arr, z_col0, ba, bat, a_scale, dt_bias, norm_w, s0, *, n_seq, seq, chunk, rows):
    nr = seq // rows
    ncb = rows // chunk
    w = BRANCH_WIDTH
    zb = z_col0 // w
    blk = pl.BlockSpec((rows, w), lambda b, r: (b * nr + r, 0))
    state = pl.BlockSpec((None, N_HEADS, HEAD_DIM, HEAD_DIM), lambda b, r: (b, 0, 0, 0))
    smem = pl.BlockSpec(memory_space=pltpu.SMEM)
    return pl.pallas_call(
        functools.partial(_dn_chunk_kernel, chunk=chunk, n_chunks=ncb),
        out_shape=(jax.ShapeDtypeStruct((n_seq * seq, w), BF16),
                   jax.ShapeDtypeStruct((n_seq, N_HEADS, HEAD_DIM, HEAD_DIM), F32)),
        grid=(n_seq, nr),
        in_specs=[smem, smem, blk, blk, blk,
                  pl.BlockSpec((rows, w), lambda b, r: (b * nr + r, zb)),
                  pl.BlockSpec((rows, LANES), lambda b, r: (b * nr + r, 0)),
                  pl.BlockSpec((ncb, 2 * N_HEADS, chunk), lambda b, r: (b * nr + r, 0, 0)),
                  pl.BlockSpec((1, HEAD_DIM), lambda b, r: (0, 0)),
                  state],
        out_specs=(blk, state),
        compiler_params=_params("parallel", "arbitrary"),
        name=f"dn_chunks_{chunk}",
    )(a_scale, dt_bias, q, k, v, z_arr, ba, bat, norm_w, s0)


def _merge_kernel(osb_ref, odn_ref, wsb_ref, wdn_ref, gsb_ref, gdn_ref, o_ref):
    a = _dot(osb_ref[...], wsb_ref[...])
    b = _dot(odn_ref[...], wdn_ref[...])
    o_ref[...] = (_sigmoid(gsb_ref[...]) * a + _sigmoid(gdn_ref[...]) * b).astype(o_ref.dtype)


def _merge(o_sb, o_dn, w_sb, w_dn, gates, *, tm, tn, name):
    n = o_sb.shape[0]
    w = BRANCH_WIDTH
    nj = D_MODEL // tn
    return pl.pallas_call(
        _merge_kernel,
        out_shape=jax.ShapeDtypeStruct((n, D_MODEL), BF16),
        grid=(n // tm, nj),
        in_specs=[pl.BlockSpec((tm, w), lambda i, j: (i, 0)),
                  pl.BlockSpec((tm, w), lambda i, j: (i, 0)),
                  pl.BlockSpec((w, tn), lambda i, j: (0, j)),
                  pl.BlockSpec((w, tn), lambda i, j: (0, j)),
                  pl.BlockSpec((tm, tn), lambda i, j: (i, j)),
                  pl.BlockSpec((tm, tn), lambda i, j: (i, j + nj))],
        out_specs=pl.BlockSpec((tm, tn), lambda i, j: (i, j)),
        compiler_params=_params("parallel", "arbitrary"),
        name=name,
    )(o_sb, o_dn, w_sb, w_dn, gates, gates)


def _out_ln_kernel(m_ref, w_ref, xn_ref, ga_ref, sc_ref, sh_ref, g_ref, b_ref, wr_ref, br_ref,
                   x1_ref, h_ref, lg_ref):
    y = _dot(m_ref[...], w_ref[...])
    x1 = _layer_norm(DEEPNORM_ALPHA * xn_ref[...] + ga_ref[...] * y, g_ref[...], b_ref[...])
    x1_ref[...] = x1
    h = x1 * (1.0 + sc_ref[...]) + sh_ref[...]
    h_ref[...] = h.astype(BF16)
    lg_ref[...] = _dot3(h, wr_ref[...]) + br_ref[...]


def _out_ln(merged, w_out, xn, mod, ln_g, ln_b, w_router, b_router, *, tm, name):
    n = merged.shape[0]
    row = lambda i: i
    blk = pl.BlockSpec((tm, D_MODEL), lambda i: (i, 0))
    vec = pl.BlockSpec((1, D_MODEL), lambda i: (0, 0))
    return pl.pallas_call(
        _out_ln_kernel,
        out_shape=(jax.ShapeDtypeStruct((n, D_MODEL), F32), jax.ShapeDtypeStruct((n, D_MODEL), BF16),
                   jax.ShapeDtypeStruct((n, LANES), F32)),
        grid=(n // tm,),
        in_specs=[blk, pl.BlockSpec((D_MODEL, D_MODEL), lambda i: (0, 0)), blk,
                  mod.spec(2, tm, row), mod.spec(4, tm, row), mod.spec(3, tm, row), vec, vec,
                  pl.BlockSpec((D_MODEL, LANES), lambda i: (0, 0)),
                  pl.BlockSpec((1, LANES), lambda i: (0, 0))],
        out_specs=(blk, blk, pl.BlockSpec((tm, LANES), lambda i: (i, 0))),
        compiler_params=_params("parallel"),
        name=name,
    )(merged, w_out, xn, mod.arr, mod.arr, mod.arr, ln_g, ln_b, w_router, b_router)


def _moe_up_kernel(be_ref, na_ref, x_ref, wg_ref, wu_ref, bg_ref, bu_ref, o_ref, wg_bf, wu_bf):
    m = pl.program_id(1)
    prev = be_ref[jnp.maximum(m - 1, 0)]
    new_weights = jnp.logical_or(m == 0, be_ref[m] != prev)

    @pl.when(m < na_ref[0])
    def _():
        @pl.when(new_weights)
        def _():
            wg_bf[...] = wg_ref[...].astype(BF16)
            wu_bf[...] = wu_ref[...].astype(BF16)

        x = x_ref[...]
        gate = jnp.minimum(_dot(x, wg_bf[...]) + bg_ref[...], SWIGLU_LIMIT)
        up = jnp.clip(_dot(x, wu_bf[...]) + bu_ref[...], -SWIGLU_LIMIT, SWIGLU_LIMIT)
        o_ref[...] = ((up + 1.0) * gate * _sigmoid(gate * SWIGLU_ALPHA)).astype(o_ref.dtype)

    @pl.when(m >= na_ref[0])
    def _():
        o_ref[...] = jnp.zeros_like(o_ref)


def _moe_down_kernel(be_ref, na_ref, a_ref, w_ref, b_ref, o_ref, w_bf):
    m = pl.program_id(1)
    prev = be_ref[jnp.maximum(m - 1, 0)]
    new_weights = jnp.logical_or(m == 0, be_ref[m] != prev)

    @pl.when(m < na_ref[0])
    def _():
        @pl.when(new_weights)
        def _():
            w_bf[...] = w_ref[...].astype(BF16)

        o_ref[...] = (_dot(a_ref[...], w_bf[...]) + b_ref[...]).astype(o_ref.dtype)

    @pl.when(m >= na_ref[0])
    def _():
        o_ref[...] = jnp.zeros_like(o_ref)


def _moe_experts(xs, block_e, n_active, w_gu, b_gu, w_dn, b_dn):
    n_rows = xs.shape[0]
    nb = n_rows // MOE_TM
    nc = D_EXPERT // MOE_TH
    b_gu3 = b_gu.reshape(N_EXPERTS, 1, 2 * D_EXPERT)
    b_dn3 = b_dn.reshape(N_EXPERTS, 1, D_MODEL)
    act = pl.pallas_call(
        _moe_up_kernel,
        out_shape=jax.ShapeDtypeStruct((n_rows, D_EXPERT), BF16),
        grid_spec=pltpu.PrefetchScalarGridSpec(
            num_scalar_prefetch=2,
            grid=(nc, nb),
            in_specs=[pl.BlockSpec((MOE_TM, D_MODEL), lambda c, m, be, na: (m, 0)),
                      pl.BlockSpec((None, D_MODEL, MOE_TH), lambda c, m, be, na: (be[m], 0, c)),
                      pl.BlockSpec((None, D_MODEL, MOE_TH), lambda c, m, be, na: (be[m], 0, nc + c)),
                      pl.BlockSpec((None, 1, MOE_TH), lambda c, m, be, na: (be[m], 0, c)),
                      pl.BlockSpec((None, 1, MOE_TH), lambda c, m, be, na: (be[m], 0, nc + c))],
            out_specs=pl.BlockSpec((MOE_TM, MOE_TH), lambda c, m, be, na: (m, c)),
            scratch_shapes=[pltpu.VMEM((D_MODEL, MOE_TH), BF16), pltpu.VMEM((D_MODEL, MOE_TH), BF16)],
        ),
        compiler_params=_params("arbitrary", "arbitrary", vmem=MOE_UP_VMEM),
        name="moe_up",
    )(block_e, n_active, xs, w_gu, w_gu, b_gu3, b_gu3)
    nj = D_MODEL // MOE_TN
    return pl.pallas_call(
        _moe_down_kernel,
        out_shape=jax.ShapeDtypeStruct((n_rows, D_MODEL), BF16),
        grid_spec=pltpu.PrefetchScalarGridSpec(
            num_scalar_prefetch=2,
            grid=(nj, nb),
            in_specs=[pl.BlockSpec((MOE_TM, D_EXPERT), lambda j, m, be, na: (m, 0)),
                      pl.BlockSpec((None, D_EXPERT, MOE_TN), lambda j, m, be, na: (be[m], 0, j)),
                      pl.BlockSpec((None, 1, MOE_TN), lambda j, m, be, na: (be[m], 0, j))],
            out_specs=pl.BlockSpec((MOE_TM, MOE_TN), lambda j, m, be, na: (m, j)),
            scratch_shapes=[pltpu.VMEM((D_EXPERT, MOE_TN), BF16)],
        ),
        compiler_params=_params("arbitrary", "arbitrary"),
        name="moe_down",
    )(block_e, n_active, act, w_dn, b_dn3)


def _rank_kernel(e_ref, rank_ref, cnt_ref, run_ref, *, rb):
    @pl.when(pl.program_id(0) == 0)
    def _():
        run_ref[...] = jnp.zeros_like(run_ref)

    expert = lax.broadcasted_iota(jnp.int32, (N_EXPERTS, rb), 0)
    onehot = expert == e_ref[...]
    r = lax.broadcasted_iota(jnp.int32, (rb, rb), 0)
    c = lax.broadcasted_iota(jnp.int32, (rb, rb), 1)
    before = _dot(jnp.where(onehot, 1.0, 0.0).astype(BF16), (r < c).astype(BF16))
    run = run_ref[...]
    rank = jnp.sum(jnp.where(onehot, before + run, 0.0), axis=0, keepdims=True)
    rank_ref[...] = rank.astype(jnp.int32)
    run = run + jnp.sum(jnp.where(onehot, 1.0, 0.0), axis=1, keepdims=True)
    run_ref[...] = run
    cnt_ref[...] = jnp.broadcast_to(run, cnt_ref.shape)


def _expert_ranks(flat_e, rb=ROUTE_BLOCK):
    n = flat_e.shape[0]
    flat_e = jnp.pad(flat_e, (0, -n % rb), constant_values=N_EXPERTS)
    nb = flat_e.shape[0] // rb
    rank, cnt = pl.pallas_call(
        functools.partial(_rank_kernel, rb=rb),
        out_shape=(jax.ShapeDtypeStruct((nb, 1, rb), jnp.int32),
                   jax.ShapeDtypeStruct((N_EXPERTS, LANES), F32)),
        grid=(nb,),
        in_specs=[pl.BlockSpec((None, 1, rb), lambda i: (i, 0, 0))],
        out_specs=(pl.BlockSpec((None, 1, rb), lambda i: (i, 0, 0)),
                   pl.BlockSpec((N_EXPERTS, LANES), lambda i: (0, 0))),
        scratch_shapes=[pltpu.VMEM((N_EXPERTS, 1), F32)],
        compiler_params=_params("arbitrary"),
        name="expert_ranks",
    )(flat_e.reshape(nb, 1, rb))
    return rank.reshape(-1)[:n], cnt[:, 0].astype(jnp.int32)


def _route(logits, tm):
    n = logits.shape[0]
    n_assign = n * TOP_K
    n_blocks = -(-n_assign // tm) + N_EXPERTS
    top_logit, top_e = lax.top_k(logits, TOP_K)
    gates = jax.nn.softmax(top_logit, axis=-1).reshape(-1)
    flat_e = top_e.reshape(-1).astype(jnp.int32)
    rank, counts = _expert_ranks(flat_e)
    padded = (counts + tm - 1) // tm * tm
    pad_end = jnp.cumsum(padded)
    pad_start = pad_end - padded
    start = jnp.cumsum(counts) - counts
    onehot = flat_e[:, None] == jnp.arange(N_EXPERTS, dtype=jnp.int32)[None, :]
    slot_of = jnp.sum(jnp.where(onehot, pad_start[None, :], 0), axis=1) + rank
    order = jnp.argsort(flat_e, stable=True).astype(jnp.int32)
    blk0 = jnp.arange(n_blocks, dtype=jnp.int32) * tm
    block_e = jnp.minimum(jnp.sum((pad_end[None, :] <= blk0[:, None]).astype(jnp.int32), axis=1), N_EXPERTS - 1)
    base_rank = blk0 - pad_start[block_e]
    n_valid = jnp.clip(counts[block_e] - base_rank, 0, tm)
    src0 = start[block_e] + base_rank
    r = jnp.arange(tm, dtype=jnp.int32)[None, :]
    valid = r < n_valid[:, None]
    src = order[jnp.clip(src0[:, None] + r, 0, n_assign - 1)]
    slot_tok = jnp.where(valid, src // TOP_K, (blk0[:, None] + r) % n).reshape(-1)
    n_active = (pad_end[-1] // tm).astype(jnp.int32).reshape(1)
    return slot_tok, block_e, n_active, slot_of.reshape(n, TOP_K), gates.reshape(n, TOP_K)


def _combine_ln_kernel(x1_ref, y_ref, gt_ref, ga_ref, g_ref, b_ref, o_ref):
    gates = gt_ref[...]
    ffn = y_ref[0].astype(F32) * gates[:, 0:1]
    for k in range(1, TOP_K):
        ffn = ffn + y_ref[k].astype(F32) * gates[:, k:k + 1]
    o_ref[...] = _layer_norm(DEEPNORM_ALPHA * x1_ref[...] + ga_ref[...] * ffn, g_ref[...], b_ref[...])


def _combine_ln(x1, y4, gates, row0, mod, ln_g, ln_b, *, tm, name):
    n = x1.shape[0]
    rb = row0 // tm
    row = lambda i: i
    blk = pl.BlockSpec((tm, D_MODEL), lambda i: (i, 0))
    vec = pl.BlockSpec((1, D_MODEL), lambda i: (0, 0))
    return pl.pallas_call(
        _combine_ln_kernel,
        out_shape=jax.ShapeDtypeStruct((n, D_MODEL), F32),
        grid=(n // tm,),
        in_specs=[blk, pl.BlockSpec((TOP_K, tm, D_MODEL), lambda i: (0, i + rb, 0)),
                  pl.BlockSpec((tm, TOP_K), lambda i: (i + rb, 0)),
                  mod.spec(5, tm, row), vec, vec],
        out_specs=blk,
        compiler_params=_params("parallel"),
        name=name,
    )(x1, y4, gates, mod.arr, ln_g, ln_b)


def _dense_front(x, mod, w, ln_in_g, ln_in_b, *, tm, tag):
    xn, h = _ln_mod(x, ln_in_g, ln_in_b, mod, tm=tm, name=f"ln_mod_{tag}")
    mm = functools.partial(_matmul, h, w["w_in_main"], tm=tm)
    q = mm(tn=1024, out_dtype=F32, n_out=BRANCH_WIDTH, col0=0, name=f"proj_q_{tag}")
    k = mm(tn=1024, out_dtype=F32, n_out=BRANCH_WIDTH, col0=BRANCH_WIDTH, name=f"proj_k_{tag}")
    v = mm(tn=1024, out_dtype=F32, n_out=BRANCH_WIDTH, col0=2 * BRANCH_WIDTH, name=f"proj_v_{tag}")
    dnz = mm(tn=1024, out_dtype=F32, n_out=4 * BRANCH_WIDTH, col0=3 * BRANCH_WIDTH, name=f"proj_dn_{tag}")
    mt = functools.partial(_matmul, h, w["w_in_tail"], tm=tm)
    gates = mt(tn=1024, out_dtype=F32, n_out=2 * D_MODEL, col0=0, name=f"proj_gate_{tag}")
    ba = mt(tn=LANES, out_dtype=F32, n_out=LANES, col0=2 * D_MODEL, name=f"proj_ba_{tag}")
    return xn, q, k, v, dnz, gates, ba


def _dense_back(o_sb, o_dn, gates, xn, mod, w, *, tm, tm_out, tag):
    merged = _merge(o_sb, o_dn, w["w_branch_sb"], w["w_branch_dn"], gates, tm=tm, tn=1024, name=f"merge_{tag}")
    return _out_ln(merged, w["w_out"], xn, mod, w["ln1_g"], w["ln1_b"], w["w_router"], w["b_router"],
                   tm=tm_out, name=f"out_ln_{tag}")


def _logit_rows(ba, n_seq, seq, chunk):
    t = ba[:, :2 * N_HEADS].reshape(n_seq * seq // chunk, chunk, 2 * N_HEADS)
    return t.transpose(0, 2, 1)


def kernel(x_prompt, x_sample, cache_k, cache_v, state_dn, state_conv, page_table, c_prompt, c_sample,
           ln_in_g, ln_in_b, w_ada, b_ada, w_in, sb_bias, conv_w, dn_a_log, dn_dt_bias, dn_norm_w,
           w_branch_sb, w_branch_dn, w_out, ln1_g, ln1_b, w_router, b_router, w_gu, b_gu,
           w_dn, b_dn, ln2_g, ln2_b):
    assert w_ada.shape[0] == DEPTH == 1
    bp, sp, d = x_prompt.shape
    bs, ss, _ = x_sample.shape
    n_p, n_s = bp * sp, bs * ss
    l = 0

    w_in_l = w_in[l]
    w_in_tail = jnp.concatenate(
        [w_in_l[:, OFF_GATE:], w_in_l[:, OFF_DN_B:OFF_GATE],
         jnp.zeros((d, LANES - 2 * N_HEADS), w_in.dtype)], axis=1).astype(BF16)
    w = {
        "w_in_main": w_in_l, "w_in_tail": w_in_tail,
        "w_branch_sb": w_branch_sb[l].astype(BF16), "w_branch_dn": w_branch_dn[l].astype(BF16),
        "w_out": w_out[l].astype(BF16),
        "ln1_g": ln1_g[l].reshape(1, d), "ln1_b": ln1_b[l].reshape(1, d),
        "w_router": jnp.pad(w_router[l], ((0, 0), (0, LANES - N_EXPERTS))),
        "b_router": jnp.pad(b_router[l], (0, LANES - N_EXPERTS)).reshape(1, LANES),
    }
    ln2g, ln2b = ln2_g[l].reshape(1, d), ln2_b[l].reshape(1, d)
    conv_w8 = jnp.pad(conv_w[l], ((0, SUBLANES - DN_CONV), (0, 0)))
    norm_w = dn_norm_w[l].reshape(1, HEAD_DIM)
    a_scale = -jnp.exp(dn_a_log[l].astype(F32))

    c_all = jnp.concatenate([c_prompt, c_sample], axis=0)
    mod_all = _matmul(c_all, w_ada[l], tm=c_all.shape[0], tn=1024, out_dtype=F32, n_out=6 * d,
                      bias=b_ada[l].reshape(1, 6 * d), silu_in=True, name="ada")
    mod_p = _Mod(mod_all[:bp], sp, n_p)
    mod_s = _Mod(mod_all[bp:], ss, n_s)

    xn_p, q_p, k_p, v_p, dnz_p, gates_p, ba_p = _dense_front(
        x_prompt.reshape(n_p, d), mod_p, w, ln_in_g, ln_in_b, tm=min(1024, n_p), tag="p")
    o_sb_p = _sb_prompt(q_p, k_p, v_p, sb_bias[l], n_seq=bp, seq=sp)
    zero_prev = jnp.zeros((bp, SUBLANES, 3 * BRANCH_WIDTH), F32)
    dq, dk, dv = _dn_prep(dnz_p, zero_prev, conv_w8, n_seq=bp, seq=sp, valid=sp, heads_per_step=1,
                          name="dn_prep_p")
    o_dn_p, s_p = _dn_chunks(dq, dk, dv, dnz_p, 3 * BRANCH_WIDTH, ba_p, _logit_rows(ba_p, bp, sp, DN_CHUNK),
                             a_scale, dn_dt_bias[l], norm_w,
                             jnp.zeros((bp, N_HEADS, HEAD_DIM, HEAD_DIM), F32),
                             n_seq=bp, seq=sp, chunk=DN_CHUNK, rows=4 * DN_CHUNK)
    x1_p, h2_p, lg_p = _dense_back(o_sb_p, o_dn_p, gates_p, xn_p, mod_p, w, tm=min(1024, n_p), tm_out=256,
                                   tag="p")

    xn_s, q_s, k_s, v_s, dnz_s, gates_s, ba_s = _dense_front(
        x_sample.reshape(n_s, d), mod_s, w, ln_in_g, ln_in_b, tm=n_s, tag="s")
    pad_q = SUBLANES - ss

    def heads_rows(t):
        t = t.reshape(bs, ss, N_HEADS, HEAD_DIM).transpose(0, 2, 1, 3)
        return (jnp.pad(t, ((0, 0), (0, 0), (0, SAMPLE_QROWS - ss), (0, 0)))
                .reshape(bs, N_HEADS * SAMPLE_QROWS, HEAD_DIM))

    def new_page(t):
        t = jnp.pad(t.reshape(bs, ss, BRANCH_WIDTH), ((0, 0), (0, PAGE_SIZE - ss), (0, 0)))
        return t.reshape(bs, PAGE_SIZE * N_HEADS, HEAD_DIM)

    n_phys = cache_k.shape[1]
    o_s = _sb_sample(heads_rows(q_s), new_page(k_s), new_page(v_s), sb_bias[l],
                     cache_k.reshape(-1, PAGE_SIZE * N_HEADS, HEAD_DIM),
                     cache_v.reshape(-1, PAGE_SIZE * N_HEADS, HEAD_DIM), page_table, l * n_phys)
    o_sb_s = (o_s.reshape(bs, N_HEADS, SAMPLE_QROWS, HEAD_DIM)[:, :, :ss].transpose(0, 2, 1, 3)
              .reshape(n_s, BRANCH_WIDTH).astype(BF16))

    def pad_rows(t):
        return jnp.pad(t.reshape(bs, ss, -1), ((0, 0), (0, pad_q), (0, 0))).reshape(bs * SUBLANES, -1)

    dnz_s8 = pad_rows(dnz_s)
    ba_s8 = pad_rows(ba_s)
    prev_s = jnp.pad(state_conv[l], ((0, 0), (SUBLANES - (DN_CONV - 1), 0), (0, 0)))
    dq, dk, dv = _dn_prep(dnz_s8, prev_s, conv_w8, n_seq=bs, seq=SUBLANES, valid=ss, heads_per_step=N_HEADS,
                          name="dn_prep_s")
    tok = jnp.arange(bs * SUBLANES) % SUBLANES
    neutral = jnp.concatenate([jnp.full((N_HEADS,), -1e30, F32), jnp.full((N_HEADS,), -1e30, F32),
                               jnp.zeros((LANES - 2 * N_HEADS,), F32)])
    ba_s8 = jnp.where((tok < ss)[:, None], ba_s8, neutral[None, :])
    o_dn_s8, s_s = _dn_chunks(dq, dk, dv, dnz_s8, 3 * BRANCH_WIDTH, ba_s8, _logit_rows(ba_s8, bs, SUBLANES, SUBLANES),
                              a_scale, dn_dt_bias[l], norm_w, state_dn[l],
                              n_seq=bs, seq=SUBLANES, chunk=SUBLANES, rows=SUBLANES)
    o_dn_s = o_dn_s8.reshape(bs, SUBLANES, BRANCH_WIDTH)[:, :ss].reshape(n_s, BRANCH_WIDTH)
    x1_s, h2_s, lg_s = _dense_back(o_sb_s, o_dn_s, gates_s, xn_s, mod_s, w, tm=n_s, tm_out=n_s, tag="s")

    logits = jnp.concatenate([lg_p[:, :N_EXPERTS], lg_s[:, :N_EXPERTS]], axis=0)
    h2 = jnp.concatenate([h2_p, h2_s], axis=0)
    slot_tok, block_e, n_active, slot_of, gates = _route(logits, MOE_TM)
    xs = h2.at[slot_tok].get(mode="promise_in_bounds")
    yb = _moe_experts(xs, block_e, n_active, w_gu[l], b_gu[l], w_dn[l], b_dn[l])
    y4 = yb.at[slot_of.T].get(mode="promise_in_bounds")
    y_p = _combine_ln(x1_p, y4, gates, 0, mod_p, ln2g, ln2b, tm=256, name="combine_ln_p")
    y_s = _combine_ln(x1_s, y4, gates, n_p, mod_s, ln2g, ln2b, tm=n_s, name="combine_ln_s")

    heads = lambda t, b, s: t.reshape(1, b, s, N_HEADS, HEAD_DIM)
    conv_p = dnz_p.reshape(bp, sp, 4 * BRANCH_WIDTH)[:, sp - (DN_CONV - 1):, :3 * BRANCH_WIDTH]
    xp_s = jnp.concatenate([state_conv[l], dnz_s.reshape(bs, ss, 4 * BRANCH_WIDTH)[:, :, :3 * BRANCH_WIDTH]], axis=1)
    conv_s = xp_s[:, -(DN_CONV - 1):]
    return (y_p.reshape(bp, sp, d), y_s.reshape(bs, ss, d),
            heads(k_p, bp, sp), heads(v_p, bp, sp), s_p[None], conv_p[None],
            heads(k_s, bs, ss), heads(v_s, bs, ss), s_s[None], conv_s[None])
```

```python
import functools
import math

import jax
import jax.numpy as jnp
from jax import lax
from jax.experimental import pallas as pl
from jax.experimental.pallas import tpu as pltpu

F32 = jnp.float32
BF16 = jnp.bfloat16

D_MODEL = 2048
HEAD_DIM = 128
N_HEADS = 8
BRANCH_WIDTH = N_HEADS * HEAD_DIM
PAGE_SIZE = 128
DN_CONV = 4
DN_CHUNK = 64
N_EXPERTS = 32
TOP_K = 4
D_EXPERT = D_MODEL
SWIGLU_LIMIT = 7.0
SWIGLU_ALPHA = 1.702
LN_EPS = 1e-5
RMS_EPS = 1e-6
L2_EPS = 1e-6
DEPTH = 1
DEEPNORM_ALPHA = (2.0 * DEPTH) ** 0.25
OFF_DN_QKV = 3 * BRANCH_WIDTH
OFF_DN_Z = OFF_DN_QKV + 3 * BRANCH_WIDTH
OFF_DN_B = OFF_DN_Z + BRANCH_WIDTH
OFF_GATE = OFF_DN_B + 2 * N_HEADS
N_IN = OFF_GATE + 2 * D_MODEL

LANES = 128
SUBLANES = 8
VMEM_LIMIT = 48 * 1024 * 1024
MOE_TM = 512
MOE_TH = 1024
MOE_UP_VMEM = 56 * 1024 * 1024
MOE_TN = 2048
ROUTE_BLOCK = 512
SB_PROMPT_TILE = 256
SB_PROMPT_HEADS = 4
SAMPLE_PAGES_PER_STEP = 16
SAMPLE_QROWS = 16


def _params(*sem, vmem=VMEM_LIMIT):
    return pltpu.CompilerParams(dimension_semantics=sem, vmem_limit_bytes=vmem)


def _dot(a, b):
    return jnp.dot(a, b, preferred_element_type=F32)


def _dot_nt(a, b):
    return lax.dot_general(a, b, (((1,), (1,)), ((), ())), preferred_element_type=F32)


def _dot_tn(a, b):
    return lax.dot_general(a, b, (((0,), (0,)), ((), ())), preferred_element_type=F32)


def _split(x):
    hi = x.astype(BF16)
    lo = (x - hi.astype(F32)).astype(BF16)
    return hi, lo


def _dot3(a, b):
    ah, al = _split(a)
    bh, bl = _split(b)
    return _dot(ah, bh) + (_dot(ah, bl) + _dot(al, bh))


def _sigmoid(x):
    return 1.0 / (1.0 + jnp.exp(-x))


def _softplus(x):
    return jnp.maximum(x, 0.0) + jnp.log1p(jnp.exp(-jnp.abs(x)))


def _layer_norm(x, g, b):
    mu = jnp.mean(x, axis=-1, keepdims=True)
    xc = x - mu
    var = jnp.mean(xc * xc, axis=-1, keepdims=True)
    return xc * lax.rsqrt(var + LN_EPS) * g + b


class _Mod:
    def __init__(self, mod, rows_per_seq, n_rows):
        n_seq = mod.shape[0]
        self.grouped = rows_per_seq % SUBLANES == 0 and rows_per_seq >= LANES
        self.rows_per_seq = rows_per_seq
        m6 = mod.reshape(n_seq, 6, D_MODEL).transpose(1, 0, 2)
        if self.grouped:
            self.arr = m6.reshape(6, n_seq, 1, D_MODEL)
        else:
            self.arr = jnp.repeat(m6, rows_per_seq, axis=1)
            assert self.arr.shape[1] == n_rows

    def spec(self, comp, tm, row_block_of):
        if self.grouped:
            bps = self.rows_per_seq // tm
            return pl.BlockSpec((None, None, 1, D_MODEL),
                                lambda *g: (comp, row_block_of(*g) // bps, 0, 0))
        return pl.BlockSpec((None, tm, D_MODEL), lambda *g: (comp, row_block_of(*g), 0))


def _mm_kernel(x_ref, w_ref, *rest, has_bias, silu_in):
    if has_bias:
        b_ref, o_ref = rest
    else:
        (o_ref,) = rest
    x = x_ref[...]
    if silu_in:
        x = x * _sigmoid(x)
    acc = _dot(x.astype(BF16), w_ref[...].astype(BF16))
    if has_bias:
        acc = acc + b_ref[...]
    o_ref[...] = acc.astype(o_ref.dtype)


def _matmul(x, w, *, tm, tn, out_dtype, n_out, col0=0, bias=None, silu_in=False, name):
    n, k = x.shape
    assert n % tm == 0 and n_out % tn == 0 and col0 % tn == 0
    cb = col0 // tn
    in_specs = [pl.BlockSpec((tm, k), lambda i, j: (i, 0)),
                pl.BlockSpec((k, tn), lambda i, j: (0, j + cb))]
    args = [x, w]
    if bias is not None:
        in_specs.append(pl.BlockSpec((1, tn), lambda i, j: (0, j + cb)))
        args.append(bias)
    return pl.pallas_call(
        functools.partial(_mm_kernel, has_bias=bias is not None, silu_in=silu_in),
        out_shape=jax.ShapeDtypeStruct((n, n_out), out_dtype),
        grid=(n // tm, n_out // tn),
        in_specs=in_specs,
        out_specs=pl.BlockSpec((tm, tn), lambda i, j: (i, j)),
        compiler_params=_params("parallel", "arbitrary"),
        name=name,
    )(*args)


def _ln_mod_kernel(x_ref, g_ref, b_ref, sc_ref, sh_ref, xn_ref, h_ref):
    xn = _layer_norm(x_ref[...], g_ref[...], b_ref[...])
    xn_ref[...] = xn
    h_ref[...] = (xn * (1.0 + sc_ref[...]) + sh_ref[...]).astype(BF16)


def _ln_mod(x, g, b, mod, *, tm, name):
    n = x.shape[0]
    row = lambda i: i
    vec = pl.BlockSpec((1, D_MODEL), lambda i: (0, 0))
    blk = pl.BlockSpec((tm, D_MODEL), lambda i: (i, 0))
    return pl.pallas_call(
        _ln_mod_kernel,
        out_shape=(jax.ShapeDtypeStruct((n, D_MODEL), F32), jax.ShapeDtypeStruct((n, D_MODEL), BF16)),
        grid=(n // tm,),
        in_specs=[blk, vec, vec, mod.spec(1, tm, row), mod.spec(0, tm, row)],
        out_specs=(blk, blk),
        compiler_params=_params("parallel"),
        name=name,
    )(x, g.reshape(1, D_MODEL), b.reshape(1, D_MODEL), mod.arr, mod.arr)


def _stick_blocks(qs, kbs, vbs, biases, log_afters, later_mat, mask):
    n = range(len(qs))
    zs = [_dot_nt(qs[i], kbs[i]) + biases[i] for i in n]
    log_keep, log_beta = [], []
    for z in zs:
        t = jnp.log(1.0 + jnp.exp(-jnp.abs(z)))
        lk = -(jnp.maximum(z, 0.0) + t)
        log_keep.append(lk if mask is None else jnp.where(mask, lk, 0.0))
        log_beta.append(jnp.minimum(z, 0.0) - t)
    parts = [_split(lk) for lk in log_keep]
    within = [_dot(hi, later_mat) + _dot(lo, later_mat) for hi, lo in parts]
    ws = []
    for i in n:
        w = jnp.exp(log_beta[i] + (within[i] + log_afters[i]))
        ws.append((w if mask is None else jnp.where(mask, w, 0.0)).astype(BF16))
    outs = [_dot(ws[i], vbs[i]) for i in n]
    new_after = [log_afters[i] + jnp.sum(log_keep[i], axis=-1, keepdims=True) for i in n]
    return outs, new_after


def _later_matrix(tk):
    r = lax.broadcasted_iota(jnp.int32, (tk, tk), 0)
    c = lax.broadcasted_iota(jnp.int32, (tk, tk), 1)
    return (r > c).astype(BF16)


def _sb_prompt_kernel(bias_ref, q_ref, k_ref, v_ref, o_ref, *, t, hp, scale):
    hg = pl.program_id(1)
    qi = pl.program_id(2)
    heads = range(hp)
    hcols = [slice(i * HEAD_DIM, (i + 1) * HEAD_DIM) for i in heads]
    biases = [bias_ref[hg * hp + i] for i in heads]
    qs = [(q_ref[:, hc] * scale).astype(BF16) for hc in hcols]
    later_mat = _later_matrix(t)
    r = lax.broadcasted_iota(jnp.int32, (t, t), 0)
    c = lax.broadcasted_iota(jnp.int32, (t, t), 1)

    def kv(j):
        rows = pl.ds(pl.multiple_of(j * t, t), t)
        return ([k_ref[rows, hc].astype(BF16) for hc in hcols],
                [v_ref[rows, hc].astype(BF16) for hc in hcols])

    kbs, vbs = kv(qi)
    zero = [jnp.zeros((t, 1), F32) for _ in heads]
    accs, afters = _stick_blocks(qs, kbs, vbs, biases, zero, later_mat, c < r)

    def body(step, carry):
        accs, afters = carry
        kbs, vbs = kv(qi - 1 - step)
        outs, afters = _stick_blocks(qs, kbs, vbs, biases, list(afters), later_mat, None)
        return tuple(a + o for a, o in zip(accs, outs)), tuple(afters)

    accs, _ = lax.fori_loop(0, qi, body, (tuple(accs), tuple(afters)))
    for i in heads:
        o_ref[:, hcols[i]] = accs[i].astype(o_ref.dtype)


def _sb_prompt(q, k, v, bias, *, n_seq, seq, t=SB_PROMPT_TILE, hp=SB_PROMPT_HEADS):
    nq = seq // t
    w = hp * HEAD_DIM
    return pl.pallas_call(
        functools.partial(_sb_prompt_kernel, t=t, hp=hp, scale=HEAD_DIM ** -0.5),
        out_shape=jax.ShapeDtypeStruct(q.shape, BF16),
        grid=(n_seq, N_HEADS // hp, nq),
        in_specs=[pl.BlockSpec(memory_space=pltpu.SMEM),
                  pl.BlockSpec((t, w), lambda b, h, i: (b * nq + i, h)),
                  pl.BlockSpec((seq, w), lambda b, h, i: (b, h)),
                  pl.BlockSpec((seq, w), lambda b, h, i: (b, h))],
        out_specs=pl.BlockSpec((t, w), lambda b, h, i: (b * nq + i, h)),
        compiler_params=_params("parallel", "parallel", "arbitrary"),
        name="sb_prompt",
    )(bias, q, k, v)


def _sb_sample_kernel(pt_ref, bias_ref, q_ref, kn_ref, vn_ref, *rest, n_pg, scale):
    k_refs = rest[:n_pg]
    v_refs = rest[n_pg:2 * n_pg]
    o_ref = rest[2 * n_pg]
    acc_ref, la_ref = rest[2 * n_pg + 1:]
    step = pl.program_id(1)
    qr = SAMPLE_QROWS
    rows = N_HEADS * qr
    later_mat = _later_matrix(PAGE_SIZE)
    bias = jnp.concatenate(
        [jnp.full((qr, 1), bias_ref[h], F32) for h in range(N_HEADS)], axis=0)
    q = (q_ref[...] * scale).astype(BF16)

    def head_rows(page_ref, h):
        return page_ref[pl.ds(h, PAGE_SIZE, stride=N_HEADS), :].astype(BF16)

    def pages(k_list, v_list, mask):
        n = len(k_list)
        hrows = [slice(h * qr, (h + 1) * qr) for h in range(N_HEADS)]
        page_lanes = [slice(i * PAGE_SIZE, (i + 1) * PAGE_SIZE) for i in range(n)]
        zs = []
        for h in range(N_HEADS):
            k_cat = jnp.concatenate([head_rows(k_ref, h) for k_ref in k_list], axis=0)
            zs.append(_dot_nt(q[hrows[h]], k_cat))
        z = jnp.concatenate(zs, axis=0) + bias
        t = jnp.log(1.0 + jnp.exp(-jnp.abs(z)))
        log_keep = -(jnp.maximum(z, 0.0) + t)
        log_beta = jnp.minimum(z, 0.0) - t
        if mask is not None:
            log_keep = jnp.where(mask, log_keep, 0.0)
        hi, lo = _split(jnp.concatenate([log_keep[:, pl_] for pl_ in page_lanes], axis=0))
        within = _dot(hi, later_mat) + _dot(lo, later_mat)
        la = la_ref[...]
        later = []
        for i in range(n):
            later.append(within[i * rows:(i + 1) * rows] + la)
            la = la + jnp.sum(log_keep[:, page_lanes[i]], axis=-1, keepdims=True)
        la_ref[...] = la
        w = jnp.exp(log_beta + jnp.concatenate(later, axis=1))
        if mask is not None:
            w = jnp.where(mask, w, 0.0)
        w = w.astype(BF16)
        for h in range(N_HEADS):
            v_cat = jnp.concatenate([head_rows(v_ref, h) for v_ref in v_list], axis=0)
            acc_ref[hrows[h], :] += _dot(w[hrows[h]], v_cat)

    @pl.when(step == 0)
    def _():
        acc_ref[...] = jnp.zeros_like(acc_ref)
        la_ref[...] = jnp.zeros_like(la_ref)
        r = lax.broadcasted_iota(jnp.int32, (rows, PAGE_SIZE), 0) % qr
        c = lax.broadcasted_iota(jnp.int32, (rows, PAGE_SIZE), 1)
        pages([kn_ref], [vn_ref], c < r)

    @pl.when(step > 0)
    def _():
        pages(list(k_refs), list(v_refs), None)

    @pl.when(step == pl.num_programs(1) - 1)
    def _():
        o_ref[...] = acc_ref[...]


def _sb_sample(q, k_new, v_new, bias, cache_k, cache_v, page_table, page0):
    bd, n_pages = page_table.shape
    n_pg = SAMPLE_PAGES_PER_STEP
    assert n_pages % n_pg == 0
    n_steps = n_pages // n_pg
    page_rows = PAGE_SIZE * N_HEADS
    rows = N_HEADS * SAMPLE_QROWS

    def page_spec(i):
        def index(b, s, pt):
            logical = n_pages - 1 - (jnp.maximum(s - 1, 0) * n_pg + i)
            return (pt[b, logical] + page0, 0, 0)
        return pl.BlockSpec((None, page_rows, HEAD_DIM), index)

    new_spec = pl.BlockSpec((None, page_rows, HEAD_DIM), lambda b, s, pt: (b, 0, 0))
    q_spec = pl.BlockSpec((None, rows, HEAD_DIM), lambda b, s, pt: (b, 0, 0))
    grid_spec = pltpu.PrefetchScalarGridSpec(
        num_scalar_prefetch=1,
        grid=(bd, n_steps + 1),
        in_specs=[pl.BlockSpec(memory_space=pltpu.SMEM), q_spec, new_spec, new_spec]
        + [page_spec(i) for i in range(n_pg)] * 2,
        out_specs=q_spec,
        scratch_shapes=[pltpu.VMEM((rows, HEAD_DIM), F32), pltpu.VMEM((rows, 1), F32)],
    )
    return pl.pallas_call(
        functools.partial(_sb_sample_kernel, n_pg=n_pg, scale=HEAD_DIM ** -0.5),
        out_shape=jax.ShapeDtypeStruct((bd, rows, HEAD_DIM), F32),
        grid_spec=grid_spec,
        compiler_params=_params("parallel", "arbitrary"),
        name="sb_sample",
    )(page_table, bias, q, k_new, v_new, *([cache_k] * n_pg), *([cache_v] * n_pg))


def _dn_prep_kernel(xq_ref, xk_ref, xv_ref, pq_ref, pk_ref, pv_ref, wq_ref, wk_ref, wv_ref,
                    oq_ref, ok_ref, ov_ref, *, seq, valid):
    width = xq_ref.shape[1]
    rows8 = lax.broadcasted_iota(jnp.int32, (SUBLANES, width), 0)

    def conv_silu(x_ref, p_ref, w_ref):
        w = w_ref[...]
        x8 = x_ref[0:SUBLANES, :]
        p8 = p_ref[...]
        head = x8 * w[DN_CONV - 1:DN_CONV]
        for k in range(1, DN_CONV):
            shifted = jnp.where(rows8 < k, pltpu.roll(p8, k, 0), pltpu.roll(x8, k, 0))
            head = head + shifted * w[DN_CONV - 1 - k:DN_CONV - k]
        if seq == SUBLANES:
            acc = head
        else:
            x = x_ref[...]
            acc = x * w[DN_CONV - 1:DN_CONV]
            for k in range(1, DN_CONV):
                acc = acc + pltpu.roll(x, k, 0) * w[DN_CONV - 1 - k:DN_CONV - k]
            acc = jnp.concatenate([head, acc[SUBLANES:]], axis=0)
        y = acc * _sigmoid(acc)
        if valid < seq:
            r = lax.broadcasted_iota(jnp.int32, y.shape, 0)
            y = jnp.where(r < valid, y, 0.0)
        return y

    def l2n(y):
        parts = [y[:, i:i + HEAD_DIM] for i in range(0, width, HEAD_DIM)]
        parts = [p * lax.rsqrt(jnp.sum(p * p, axis=-1, keepdims=True) + L2_EPS) for p in parts]
        return parts[0] if len(parts) == 1 else jnp.concatenate(parts, axis=1)

    oq_ref[...] = l2n(conv_silu(xq_ref, pq_ref, wq_ref)) * (HEAD_DIM ** -0.5)
    ok_ref[...] = l2n(conv_silu(xk_ref, pk_ref, wk_ref))
    ov_ref[...] = conv_silu(xv_ref, pv_ref, wv_ref)


def _dn_prep(x, prev, conv_w, *, n_seq, seq, valid, heads_per_step, name):
    nh = N_HEADS // heads_per_step
    wd = heads_per_step * HEAD_DIM
    xs = [pl.BlockSpec((seq, wd), lambda b, h, s=s: (b, s * nh + h)) for s in range(3)]
    ps = [pl.BlockSpec((None, SUBLANES, wd), lambda b, h, s=s: (b, 0, s * nh + h)) for s in range(3)]
    ws = [pl.BlockSpec((SUBLANES, wd), lambda b, h, s=s: (0, s * nh + h)) for s in range(3)]
    out = pl.BlockSpec((seq, wd), lambda b, h: (b, h))
    shape = jax.ShapeDtypeStruct((n_seq * seq, BRANCH_WIDTH), F32)
    return pl.pallas_call(
        functools.partial(_dn_prep_kernel, seq=seq, valid=valid),
        out_shape=(shape, shape, shape),
        grid=(n_seq, nh),
        in_specs=xs + ps + ws,
        out_specs=(out, out, out),
        compiler_params=_params("parallel", "parallel"),
        name=name,
    )(x, x, x, prev, prev, prev, conv_w, conv_w, conv_w)


def _mm3(x, y):
    (xh, xl), (yh, yl) = x, y
    return _dot(xh, yh) + (_dot(xh, yl) + _dot(xl, yh))


def _mm3_shared(lefts, right):
    rows = [x[0].shape[0] for x in lefts]
    if any(n % (2 * SUBLANES) for n in rows):
        return [_mm3(x, right) for x in lefts]
    yh, yl = right
    a = _dot(jnp.concatenate([part for x in lefts for part in x], axis=0), yh)
    b = _dot(jnp.concatenate([x[0] for x in lefts], axis=0), yl)
    out, ra, rb = [], 0, 0
    for n in rows:
        out.append(a[ra:ra + n] + (b[rb:rb + n] + a[ra + n:ra + 2 * n]))
        ra += 2 * n
        rb += n
    return out


def _unit_lower_inverse_each(mats, c):
    r = lax.broadcasted_iota(jnp.int32, (c, c), 0)
    col = lax.broadcasted_iota(jnp.int32, (c, c), 1)
    eye = jnp.where(r == col, 1.0, 0.0)
    p = [eye - a for a in mats]
    a_s = [_split(a) for a in mats]
    power = [_mm3_shared([s], s)[0] for s in a_s]
    for _ in range(int(math.log2(c)) - 2):
        pw_s = [_split(x) for x in power]
        p_s = [_split(x) for x in p]
        both = [_mm3_shared([ws, ps], ws) for ps, ws in zip(p_s, pw_s)]
        power = [b[0] for b in both]
        p = [x + b[1] for x, b in zip(p, both)]
    pw_s = [_split(x) for x in power]
    p_s = [_split(x) for x in p]
    return [x + _mm3_shared([ps], ws)[0] for x, ps, ws in zip(p, p_s, pw_s)]


def _dn_chunk_kernel(ascale_ref, dtb_ref, q_ref, k_ref, v_ref, z_ref, ba_ref, bat_ref, nw_ref, s0_ref,
                     o_ref, s_ref, *, chunk, n_chunks):
    c = chunk
    heads = range(N_HEADS)

    @pl.when(pl.program_id(1) == 0)
    def _():
        s_ref[...] = s0_ref[...]

    r = lax.broadcasted_iota(jnp.int32, (c, c), 0)
    col = lax.broadcasted_iota(jnp.int32, (c, c), 1)
    tri = r >= col
    strict = r > col
    nw = nw_ref[...]

    def one_chunk(ci, carry):
        rows = pl.ds(pl.multiple_of(ci * c, c), c)
        bat = bat_ref[ci]
        hcols = [slice(h * HEAD_DIM, (h + 1) * HEAD_DIM) for h in heads]
        q = [q_ref[rows, hc] for hc in hcols]
        k = [k_ref[rows, hc] for hc in hcols]
        v = [v_ref[rows, hc] for hc in hcols]
        kb = [x.astype(BF16) for x in k]
        if c % (2 * SUBLANES) == 0:
            both = [_dot_nt(jnp.concatenate([kb[h], q[h].astype(BF16)], axis=0), kb[h]) for h in heads]
            kk = [x[:c] for x in both]
            qk = [x[c:] for x in both]
        else:
            kk = [_dot_nt(x, x) for x in kb]
            qk = [_dot_nt(x.astype(BF16), y) for x, y in zip(q, kb)]
        beta, gc_col, g_last, decay, e_col = [], [], [], [], []
        for h in heads:
            a_scale = ascale_ref[h]
            beta.append(_sigmoid(ba_ref[rows, h:h + 1]))
            g_col = a_scale * _softplus(ba_ref[rows, N_HEADS + h:N_HEADS + h + 1] + dtb_ref[h])
            g_row = a_scale * _softplus(bat[N_HEADS + h:N_HEADS + h + 1, :] + dtb_ref[h])
            gcc = jnp.sum(jnp.where(tri, jnp.broadcast_to(g_row, (c, c)), 0.0), axis=1, keepdims=True)
            gcr = jnp.sum(jnp.where(r <= col, jnp.broadcast_to(g_col, (c, c)), 0.0), axis=0, keepdims=True)
            gc_col.append(gcc)
            g_last.append(jnp.sum(g_row, axis=1, keepdims=True))
            decay.append(jnp.exp(jnp.where(tri, gcc - gcr, -1e30)))
            e_col.append(jnp.exp(gcc))
        a_low = [jnp.where(strict, kk[h] * decay[h], 0.0) * beta[h] for h in heads]
        t_inv = _unit_lower_inverse_each(a_low, c)
        rhs = [jnp.concatenate([v[h] * beta[h], k[h] * (beta[h] * e_col[h])], axis=1) for h in heads]
        t_s = [_split(x) for x in t_inv]
        rhs_s = [_split(x) for x in rhs]
        sol = [_mm3_shared([t_s[h]], rhs_s[h])[0] for h in heads]
        state = [s_ref[h] for h in heads]
        sb = [x.astype(BF16) for x in state]
        if c % (2 * SUBLANES) == 0:
            ws = [_dot(jnp.concatenate([sol[h][:, HEAD_DIM:].astype(BF16), (q[h] * e_col[h]).astype(BF16)], axis=0),
                       sb[h]) for h in heads]
            u = [sol[h][:, :HEAD_DIM] - ws[h][:c] for h in heads]
            o_state = [ws[h][c:] for h in heads]
        else:
            u = [sol[h][:, :HEAD_DIM] - _dot(sol[h][:, HEAD_DIM:].astype(BF16), sb[h]) for h in heads]
            o_state = [_dot((q[h] * e_col[h]).astype(BF16), sb[h]) for h in heads]
        ub = [x.astype(BF16) for x in u]
        o = [o_state[h] + _dot(jnp.where(tri, qk[h] * decay[h], 0.0).astype(BF16), ub[h]) for h in heads]
        k_end = [(k[h] * jnp.exp(g_last[h] - gc_col[h])).astype(BF16) for h in heads]
        s_new = [state[h] * jnp.exp(g_last[h]) + _dot_tn(k_end[h], ub[h]) for h in heads]
        for h in heads:
            s_ref[h] = s_new[h]
            on = o[h] * lax.rsqrt(jnp.mean(o[h] * o[h], axis=-1, keepdims=True) + RMS_EPS)
            zz = z_ref[rows, hcols[h]]
            o_ref[rows, hcols[h]] = (on * nw * (zz * _sigmoid(zz))).astype(o_ref.dtype)
        return carry

    lax.fori_loop(0, n_chunks, one_chunk, 0)


def _dn_chunks(q, k, v, z_arr, z_col0, ba, bat, a_scale, dt_bias, norm_w, s0, *, n_seq, seq, chunk, rows):
    nr = seq // rows
    ncb = rows // chunk
    w = BRANCH_WIDTH
    zb = z_col0 // w
    blk = pl.BlockSpec((rows, w), lambda b, r: (b * nr + r, 0))
    state = pl.BlockSpec((None, N_HEADS, HEAD_DIM, HEAD_DIM), lambda b, r: (b, 0, 0, 0))
    smem = pl.BlockSpec(memory_space=pltpu.SMEM)
    return pl.pallas_call(
        functools.partial(_dn_chunk_kernel, chunk=chunk, n_chunks=ncb),
        out_shape=(jax.ShapeDtypeStruct((n_seq * seq, w), BF16),
                   jax.ShapeDtypeStruct((n_seq, N_HEADS, HEAD_DIM, HEAD_DIM), F32)),
        grid=(n_seq, nr),
        in_specs=[smem, smem, blk, blk, blk,
                  pl.BlockSpec((rows, w), lambda b, r: (b * nr + r, zb)),
                  pl.BlockSpec((rows, LANES), lambda b, r: (b * nr + r, 0)),
                  pl.BlockSpec((ncb, 2 * N_HEADS, chunk), lambda b, r: (b * nr + r, 0, 0)),
                  pl.BlockSpec((1, HEAD_DIM), lambda b, r: (0, 0)),
                  state],
        out_specs=(blk, state),
        compiler_params=_params("parallel", "arbitrary"),
        name=f"dn_chunks_{chunk}",
    )(a_scale, dt_bias, q, k, v, z_arr, ba, bat, norm_w, s0)


def _merge_kernel(osb_ref, odn_ref, wsb_ref, wdn_ref, gsb_ref, gdn_ref, o_ref):
    a = _dot(osb_ref[...], wsb_ref[...])
    b = _dot(odn_ref[...], wdn_ref[...])
    o_ref[...] = (_sigmoid(gsb_ref[...]) * a + _sigmoid(gdn_ref[...]) * b).astype(o_ref.dtype)


def _merge(o_sb, o_dn, w_sb, w_dn, gates, *, tm, tn, name):
    n = o_sb.shape[0]
    w = BRANCH_WIDTH
    nj = D_MODEL // tn
    return pl.pallas_call(
        _merge_kernel,
        out_shape=jax.ShapeDtypeStruct((n, D_MODEL), BF16),
        grid=(n // tm, nj),
        in_specs=[pl.BlockSpec((tm, w), lambda i, j: (i, 0)),
                  pl.BlockSpec((tm, w), lambda i, j: (i, 0)),
                  pl.BlockSpec((w, tn), lambda i, j: (0, j)),
                  pl.BlockSpec((w, tn), lambda i, j: (0, j)),
                  pl.BlockSpec((tm, tn), lambda i, j: (i, j)),
                  pl.BlockSpec((tm, tn), lambda i, j: (i, j + nj))],
        out_specs=pl.BlockSpec((tm, tn), lambda i, j: (i, j)),
        compiler_params=_params("parallel", "arbitrary"),
        name=name,
    )(o_sb, o_dn, w_sb, w_dn, gates, gates)


def _out_ln_kernel(m_ref, w_ref, xn_ref, ga_ref, sc_ref, sh_ref, g_ref, b_ref, wr_ref, br_ref,
                   x1_ref, h_ref, lg_ref):
    y = _dot(m_ref[...], w_ref[...])
    x1 = _layer_norm(DEEPNORM_ALPHA * xn_ref[...] + ga_ref[...] * y, g_ref[...], b_ref[...])
    x1_ref[...] = x1
    h = x1 * (1.0 + sc_ref[...]) + sh_ref[...]
    h_ref[...] = h.astype(BF16)
    lg_ref[...] = _dot3(h, wr_ref[...]) + br_ref[...]


def _out_ln(merged, w_out, xn, mod, ln_g, ln_b, w_router, b_router, *, tm, name):
    n = merged.shape[0]
    row = lambda i: i
    blk = pl.BlockSpec((tm, D_MODEL), lambda i: (i, 0))
    vec = pl.BlockSpec((1, D_MODEL), lambda i: (0, 0))
    return pl.pallas_call(
        _out_ln_kernel,
        out_shape=(jax.ShapeDtypeStruct((n, D_MODEL), F32), jax.ShapeDtypeStruct((n, D_MODEL), BF16),
                   jax.ShapeDtypeStruct((n, LANES), F32)),
        grid=(n // tm,),
        in_specs=[blk, pl.BlockSpec((D_MODEL, D_MODEL), lambda i: (0, 0)), blk,
                  mod.spec(2, tm, row), mod.spec(4, tm, row), mod.spec(3, tm, row), vec, vec,
                  pl.BlockSpec((D_MODEL, LANES), lambda i: (0, 0)),
                  pl.BlockSpec((1, LANES), lambda i: (0, 0))],
        out_specs=(blk, blk, pl.BlockSpec((tm, LANES), lambda i: (i, 0))),
        compiler_params=_params("parallel"),
        name=name,
    )(merged, w_out, xn, mod.arr, mod.arr, mod.arr, ln_g, ln_b, w_router, b_router)


def _moe_up_kernel(be_ref, na_ref, x_ref, wg_ref, wu_ref, bg_ref, bu_ref, o_ref, wg_bf, wu_bf):
    m = pl.program_id(1)
    prev = be_ref[jnp.maximum(m - 1, 0)]
    new_weights = jnp.logical_or(m == 0, be_ref[m] != prev)

    @pl.when(m < na_ref[0])
    def _():
        @pl.when(new_weights)
        def _():
            wg_bf[...] = wg_ref[...].astype(BF16)
            wu_bf[...] = wu_ref[...].astype(BF16)

        x = x_ref[...]
        gate = jnp.minimum(_dot(x, wg_bf[...]) + bg_ref[...], SWIGLU_LIMIT)
        up = jnp.clip(_dot(x, wu_bf[...]) + bu_ref[...], -SWIGLU_LIMIT, SWIGLU_LIMIT)
        o_ref[...] = ((up + 1.0) * gate * _sigmoid(gate * SWIGLU_ALPHA)).astype(o_ref.dtype)

    @pl.when(m >= na_ref[0])
    def _():
        o_ref[...] = jnp.zeros_like(o_ref)


def _moe_down_kernel(be_ref, na_ref, a_ref, w_ref, b_ref, o_ref, w_bf):
    m = pl.program_id(1)
    prev = be_ref[jnp.maximum(m - 1, 0)]
    new_weights = jnp.logical_or(m == 0, be_ref[m] != prev)

    @pl.when(m < na_ref[0])
    def _():
        @pl.when(new_weights)
        def _():
            w_bf[...] = w_ref[...].astype(BF16)

        o_ref[...] = (_dot(a_ref[...], w_bf[...]) + b_ref[...]).astype(o_ref.dtype)

    @pl.when(m >= na_ref[0])
    def _():
        o_ref[...] = jnp.zeros_like(o_ref)


def _moe_experts(xs, block_e, n_active, w_gu, b_gu, w_dn, b_dn):
    n_rows = xs.shape[0]
    nb = n_rows // MOE_TM
    nc = D_EXPERT // MOE_TH
    b_gu3 = b_gu.reshape(N_EXPERTS, 1, 2 * D_EXPERT)
    b_dn3 = b_dn.reshape(N_EXPERTS, 1, D_MODEL)
    act = pl.pallas_call(
        _moe_up_kernel,
        out_shape=jax.ShapeDtypeStruct((n_rows, D_EXPERT), BF16),
        grid_spec=pltpu.PrefetchScalarGridSpec(
            num_scalar_prefetch=2,
            grid=(nc, nb),
            in_specs=[pl.BlockSpec((MOE_TM, D_MODEL), lambda c, m, be, na: (m, 0)),
                      pl.BlockSpec((None, D_MODEL, MOE_TH), lambda c, m, be, na: (be[m], 0, c)),
                      pl.BlockSpec((None, D_MODEL, MOE_TH), lambda c, m, be, na: (be[m], 0, nc + c)),
                      pl.BlockSpec((None, 1, MOE_TH), lambda c, m, be, na: (be[m], 0, c)),
                      pl.BlockSpec((None, 1, MOE_TH), lambda c, m, be, na: (be[m], 0, nc + c))],
            out_specs=pl.BlockSpec((MOE_TM, MOE_TH), lambda c, m, be, na: (m, c)),
            scratch_shapes=[pltpu.VMEM((D_MODEL, MOE_TH), BF16), pltpu.VMEM((D_MODEL, MOE_TH), BF16)],
        ),
        compiler_params=_params("arbitrary", "arbitrary", vmem=MOE_UP_VMEM),
        name="moe_up",
    )(block_e, n_active, xs, w_gu, w_gu, b_gu3, b_gu3)
    nj = D_MODEL // MOE_TN
    return pl.pallas_call(
        _moe_down_kernel,
        out_shape=jax.ShapeDtypeStruct((n_rows, D_MODEL), BF16),
        grid_spec=pltpu.PrefetchScalarGridSpec(
            num_scalar_prefetch=2,
            grid=(nj, nb),
            in_specs=[pl.BlockSpec((MOE_TM, D_EXPERT), lambda j, m, be, na: (m, 0)),
                      pl.BlockSpec((None, D_EXPERT, MOE_TN), lambda j, m, be, na: (be[m], 0, j)),
                      pl.BlockSpec((None, 1, MOE_TN), lambda j, m, be, na: (be[m], 0, j))],
            out_specs=pl.BlockSpec((MOE_TM, MOE_TN), lambda j, m, be, na: (m, j)),
            scratch_shapes=[pltpu.VMEM((D_EXPERT, MOE_TN), BF16)],
        ),
        compiler_params=_params("arbitrary", "arbitrary", vmem=MOE_UP_VMEM),
        name="moe_down",
    )(block_e, n_active, act, w_dn, b_dn3)


def _rank_kernel(e_ref, rank_ref, cnt_ref, run_ref, *, rb):
    @pl.when(pl.program_id(0) == 0)
    def _():
        run_ref[...] = jnp.zeros_like(run_ref)

    expert = lax.broadcasted_iota(jnp.int32, (N_EXPERTS, rb), 0)
    onehot = expert == e_ref[...]
    r = lax.broadcasted_iota(jnp.int32, (rb, rb), 0)
    c = lax.broadcasted_iota(jnp.int32, (rb, rb), 1)
    before = _dot(jnp.where(onehot, 1.0, 0.0).astype(BF16), (r < c).astype(BF16))
    run = run_ref[...]
    rank = jnp.sum(jnp.where(onehot, before + run, 0.0), axis=0, keepdims=True)
    rank_ref[...] = rank.astype(jnp.int32)
    run = run + jnp.sum(jnp.where(onehot, 1.0, 0.0), axis=1, keepdims=True)
    run_ref[...] = run
    cnt_ref[...] = jnp.broadcast_to(run, cnt_ref.shape)


def _expert_ranks(flat_e, rb=ROUTE_BLOCK):
    n = flat_e.shape[0]
    flat_e = jnp.pad(flat_e, (0, -n % rb), constant_values=N_EXPERTS)
    nb = flat_e.shape[0] // rb
    rank, cnt = pl.pallas_call(
        functools.partial(_rank_kernel, rb=rb),
        out_shape=(jax.ShapeDtypeStruct((nb, 1, rb), jnp.int32),
                   jax.ShapeDtypeStruct((N_EXPERTS, LANES), F32)),
        grid=(nb,),
        in_specs=[pl.BlockSpec((None, 1, rb), lambda i: (i, 0, 0))],
        out_specs=(pl.BlockSpec((None, 1, rb), lambda i: (i, 0, 0)),
                   pl.BlockSpec((N_EXPERTS, LANES), lambda i: (0, 0))),
        scratch_shapes=[pltpu.VMEM((N_EXPERTS, 1), F32)],
        compiler_params=_params("arbitrary"),
        name="expert_ranks",
    )(flat_e.reshape(nb, 1, rb))
    return rank.reshape(-1)[:n], cnt[:, 0].astype(jnp.int32)


def _route(logits, tm):
    n = logits.shape[0]
    n_assign = n * TOP_K
    n_blocks = -(-n_assign // tm) + N_EXPERTS
    top_logit, top_e = lax.top_k(logits, TOP_K)
    gates = jax.nn.softmax(top_logit, axis=-1).reshape(-1)
    flat_e = top_e.reshape(-1).astype(jnp.int32)
    rank, counts = _expert_ranks(flat_e)
    padded = (counts + tm - 1) // tm * tm
    pad_end = jnp.cumsum(padded)
    pad_start = pad_end - padded
    start = jnp.cumsum(counts) - counts
    onehot = flat_e[:, None] == jnp.arange(N_EXPERTS, dtype=jnp.int32)[None, :]
    slot_of = jnp.sum(jnp.where(onehot, pad_start[None, :], 0), axis=1) + rank
    order = jnp.argsort(flat_e, stable=True).astype(jnp.int32)
    blk0 = jnp.arange(n_blocks, dtype=jnp.int32) * tm
    block_e = jnp.minimum(jnp.sum((pad_end[None, :] <= blk0[:, None]).astype(jnp.int32), axis=1), N_EXPERTS - 1)
    base_rank = blk0 - pad_start[block_e]
    n_valid = jnp.clip(counts[block_e] - base_rank, 0, tm)
    src0 = start[block_e] + base_rank
    r = jnp.arange(tm, dtype=jnp.int32)[None, :]
    valid = r < n_valid[:, None]
    src = order[jnp.clip(src0[:, None] + r, 0, n_assign - 1)]
    slot_tok = jnp.where(valid, src // TOP_K, (blk0[:, None] + r) % n).reshape(-1)
    n_active = (pad_end[-1] // tm).astype(jnp.int32).reshape(1)
    return slot_tok, block_e, n_active, slot_of.reshape(n, TOP_K), gates.reshape(n, TOP_K)


def _combine_ln_kernel(x1_ref, y_ref, gt_ref, ga_ref, g_ref, b_ref, o_ref):
    gates = gt_ref[...]
    ffn = y_ref[0].astype(F32) * gates[:, 0:1]
    for k in range(1, TOP_K):
        ffn = ffn + y_ref[k].astype(F32) * gates[:, k:k + 1]
    o_ref[...] = _layer_norm(DEEPNORM_ALPHA * x1_ref[...] + ga_ref[...] * ffn, g_ref[...], b_ref[...])


def _combine_ln(x1, y4, gates, row0, mod, ln_g, ln_b, *, tm, name):
    n = x1.shape[0]
    rb = row0 // tm
    row = lambda i: i
    blk = pl.BlockSpec((tm, D_MODEL), lambda i: (i, 0))
    vec = pl.BlockSpec((1, D_MODEL), lambda i: (0, 0))
    return pl.pallas_call(
        _combine_ln_kernel,
        out_shape=jax.ShapeDtypeStruct((n, D_MODEL), F32),
        grid=(n // tm,),
        in_specs=[blk, pl.BlockSpec((TOP_K, tm, D_MODEL), lambda i: (0, i + rb, 0)),
                  pl.BlockSpec((tm, TOP_K), lambda i: (i + rb, 0)),
                  mod.spec(5, tm, row), vec, vec],
        out_specs=blk,
        compiler_params=_params("parallel"),
        name=name,
    )(x1, y4, gates, mod.arr, ln_g, ln_b)


def _dense_front(x, mod, w, ln_in_g, ln_in_b, *, tm, tag):
    xn, h = _ln_mod(x, ln_in_g, ln_in_b, mod, tm=tm, name=f"ln_mod_{tag}")
    mm = functools.partial(_matmul, h, w["w_in_main"], tm=tm)
    q = mm(tn=1024, out_dtype=F32, n_out=BRANCH_WIDTH, col0=0, name=f"proj_q_{tag}")
    k = mm(tn=1024, out_dtype=F32, n_out=BRANCH_WIDTH, col0=BRANCH_WIDTH, name=f"proj_k_{tag}")
    v = mm(tn=1024, out_dtype=F32, n_out=BRANCH_WIDTH, col0=2 * BRANCH_WIDTH, name=f"proj_v_{tag}")
    dnz = mm(tn=1024, out_dtype=F32, n_out=4 * BRANCH_WIDTH, col0=3 * BRANCH_WIDTH, name=f"proj_dn_{tag}")
    mt = functools.partial(_matmul, h, w["w_in_tail"], tm=tm)
    gates = mt(tn=1024, out_dtype=F32, n_out=2 * D_MODEL, col0=0, name=f"proj_gate_{tag}")
    ba = mt(tn=LANES, out_dtype=F32, n_out=LANES, col0=2 * D_MODEL, name=f"proj_ba_{tag}")
    return xn, q, k, v, dnz, gates, ba


def _dense_back(o_sb, o_dn, gates, xn, mod, w, *, tm, tm_out, tag):
    merged = _merge(o_sb, o_dn, w["w_branch_sb"], w["w_branch_dn"], gates, tm=tm, tn=1024, name=f"merge_{tag}")
    return _out_ln(merged, w["w_out"], xn, mod, w["ln1_g"], w["ln1_b"], w["w_router"], w["b_router"],
                   tm=tm_out, name=f"out_ln_{tag}")


def _logit_rows(ba, n_seq, seq, chunk):
    t = ba[:, :2 * N_HEADS].reshape(n_seq * seq // chunk, chunk, 2 * N_HEADS)
    return t.transpose(0, 2, 1)


def kernel(x_prompt, x_sample, cache_k, cache_v, state_dn, state_conv, page_table, c_prompt, c_sample,
           ln_in_g, ln_in_b, w_ada, b_ada, w_in, sb_bias, conv_w, dn_a_log, dn_dt_bias, dn_norm_w,
           w_branch_sb, w_branch_dn, w_out, ln1_g, ln1_b, w_router, b_router, w_gu, b_gu,
           w_dn, b_dn, ln2_g, ln2_b):
    assert w_ada.shape[0] == DEPTH == 1
    bp, sp, d = x_prompt.shape
    bs, ss, _ = x_sample.shape
    n_p, n_s = bp * sp, bs * ss
    l = 0

    w_in_l = w_in[l]
    w_in_tail = jnp.concatenate(
        [w_in_l[:, OFF_GATE:], w_in_l[:, OFF_DN_B:OFF_GATE],
         jnp.zeros((d, LANES - 2 * N_HEADS), w_in.dtype)], axis=1).astype(BF16)
    w = {
        "w_in_main": w_in_l, "w_in_tail": w_in_tail,
        "w_branch_sb": w_branch_sb[l].astype(BF16), "w_branch_dn": w_branch_dn[l].astype(BF16),
        "w_out": w_out[l].astype(BF16),
        "ln1_g": ln1_g[l].reshape(1, d), "ln1_b": ln1_b[l].reshape(1, d),
        "w_router": jnp.pad(w_router[l], ((0, 0), (0, LANES - N_EXPERTS))),
        "b_router": jnp.pad(b_router[l], (0, LANES - N_EXPERTS)).reshape(1, LANES),
    }
    ln2g, ln2b = ln2_g[l].reshape(1, d), ln2_b[l].reshape(1, d)
    conv_w8 = jnp.pad(conv_w[l], ((0, SUBLANES - DN_CONV), (0, 0)))
    norm_w = dn_norm_w[l].reshape(1, HEAD_DIM)
    a_scale = -jnp.exp(dn_a_log[l].astype(F32))

    c_all = jnp.concatenate([c_prompt, c_sample], axis=0)
    mod_all = _matmul(c_all, w_ada[l], tm=c_all.shape[0], tn=1024, out_dtype=F32, n_out=6 * d,
                      bias=b_ada[l].reshape(1, 6 * d), silu_in=True, name="ada")
    mod_p = _Mod(mod_all[:bp], sp, n_p)
    mod_s = _Mod(mod_all[bp:], ss, n_s)

    xn_p, q_p, k_p, v_p, dnz_p, gates_p, ba_p = _dense_front(
        x_prompt.reshape(n_p, d), mod_p, w, ln_in_g, ln_in_b, tm=min(1024, n_p), tag="p")
    o_sb_p = _sb_prompt(q_p, k_p, v_p, sb_bias[l], n_seq=bp, seq=sp)
    zero_prev = jnp.zeros((bp, SUBLANES, 3 * BRANCH_WIDTH), F32)
    dq, dk, dv = _dn_prep(dnz_p, zero_prev, conv_w8, n_seq=bp, seq=sp, valid=sp, heads_per_step=1,
                          name="dn_prep_p")
    o_dn_p, s_p = _dn_chunks(dq, dk, dv, dnz_p, 3 * BRANCH_WIDTH, ba_p, _logit_rows(ba_p, bp, sp, DN_CHUNK),
                             a_scale, dn_dt_bias[l], norm_w,
                             jnp.zeros((bp, N_HEADS, HEAD_DIM, HEAD_DIM), F32),
                             n_seq=bp, seq=sp, chunk=DN_CHUNK, rows=4 * DN_CHUNK)
    x1_p, h2_p, lg_p = _dense_back(o_sb_p, o_dn_p, gates_p, xn_p, mod_p, w, tm=min(1024, n_p), tm_out=256,
                                   tag="p")

    xn_s, q_s, k_s, v_s, dnz_s, gates_s, ba_s = _dense_front(
        x_sample.reshape(n_s, d), mod_s, w, ln_in_g, ln_in_b, tm=n_s, tag="s")
    pad_q = SUBLANES - ss

    def heads_rows(t):
        t = t.reshape(bs, ss, N_HEADS, HEAD_DIM).transpose(0, 2, 1, 3)
        return (jnp.pad(t, ((0, 0), (0, 0), (0, SAMPLE_QROWS - ss), (0, 0)))
                .reshape(bs, N_HEADS * SAMPLE_QROWS, HEAD_DIM))

    def new_page(t):
        t = jnp.pad(t.reshape(bs, ss, BRANCH_WIDTH), ((0, 0), (0, PAGE_SIZE - ss), (0, 0)))
        return t.reshape(bs, PAGE_SIZE * N_HEADS, HEAD_DIM)

    n_phys = cache_k.shape[1]
    o_s = _sb_sample(heads_rows(q_s), new_page(k_s), new_page(v_s), sb_bias[l],
                     cache_k.reshape(-1, PAGE_SIZE * N_HEADS, HEAD_DIM),
                     cache_v.reshape(-1, PAGE_SIZE * N_HEADS, HEAD_DIM), page_table, l * n_phys)
    o_sb_s = (o_s.reshape(bs, N_HEADS, SAMPLE_QROWS, HEAD_DIM)[:, :, :ss].transpose(0, 2, 1, 3)
              .reshape(n_s, BRANCH_WIDTH).astype(BF16))

    def pad_rows(t):
        return jnp.pad(t.reshape(bs, ss, -1), ((0, 0), (0, pad_q), (0, 0))).reshape(bs * SUBLANES, -1)

    dnz_s8 = pad_rows(dnz_s)
    ba_s8 = pad_rows(ba_s)
    prev_s = jnp.pad(state_conv[l], ((0, 0), (SUBLANES - (DN_CONV - 1), 0), (0, 0)))
    dq, dk, dv = _dn_prep(dnz_s8, prev_s, conv_w8, n_seq=bs, seq=SUBLANES, valid=ss, heads_per_step=N_HEADS,
                          name="dn_prep_s")
    tok = jnp.arange(bs * SUBLANES) % SUBLANES
    neutral = jnp.concatenate([jnp.full((N_HEADS,), -1e30, F32), jnp.full((N_HEADS,), -1e30, F32),
                               jnp.zeros((LANES - 2 * N_HEADS,), F32)])
    ba_s8 = jnp.where((tok < ss)[:, None], ba_s8, neutral[None, :])
    o_dn_s8, s_s = _dn_chunks(dq, dk, dv, dnz_s8, 3 * BRANCH_WIDTH, ba_s8, _logit_rows(ba_s8, bs, SUBLANES, SUBLANES),
                              a_scale, dn_dt_bias[l], norm_w, state_dn[l],
                              n_seq=bs, seq=SUBLANES, chunk=SUBLANES, rows=SUBLANES)
    o_dn_s = o_dn_s8.reshape(bs, SUBLANES, BRANCH_WIDTH)[:, :ss].reshape(n_s, BRANCH_WIDTH)
    x1_s, h2_s, lg_s = _dense_back(o_sb_s, o_dn_s, gates_s, xn_s, mod_s, w, tm=n_s, tm_out=n_s, tag="s")

    logits = jnp.concatenate([lg_p[:, :N_EXPERTS], lg_s[:, :N_EXPERTS]], axis=0)
    h2 = jnp.concatenate([h2_p, h2_s], axis=0)
    slot_tok, block_e, n_active, slot_of, gates = _route(logits, MOE_TM)
    xs = h2.at[slot_tok].get(mode="promise_in_bounds")
    yb = _moe_experts(xs, block_e, n_active, w_gu[l], b_gu[l], w_dn[l], b_dn[l])
    y4 = yb.at[slot_of.T].get(mode="promise_in_bounds")
    y_p = _combine_ln(x1_p, y4, gates, 0, mod_p, ln2g, ln2b, tm=256, name="combine_ln_p")
    y_s = _combine_ln(x1_s, y4, gates, n_p, mod_s, ln2g, ln2b, tm=n_s, name="combine_ln_s")

    heads = lambda t, b, s: t.reshape(1, b, s, N_HEADS, HEAD_DIM)
    conv_p = dnz_p.reshape(bp, sp, 4 * BRANCH_WIDTH)[:, sp - (DN_CONV - 1):, :3 * BRANCH_WIDTH]
    xp_s = jnp.concatenate([state_conv[l], dnz_s.reshape(bs, ss, 4 * BRANCH_WIDTH)[:, :, :3 * BRANCH_WIDTH]], axis=1)
    conv_s = xp_s[:, -(DN_CONV - 1):]
    return (y_p.reshape(bp, sp, d), y_s.reshape(bs, ss, d),
            heads(k_p, bp, sp), heads(v_p, bp, sp), s_p[None], conv_p[None],
            heads(k_s, bs, ss), heads(v_s, bs, ss), s_s[None], conv_s[None])
```
